```python
import math
import jax, jax.numpy as jnp
from jax import lax
import numpy as np

D_MODEL = 1024
BATCH = 16
SEQ = 4096
DEPTH = 2

ATT_HEADS = 8
ATT_KV_HEADS = 2
ATT_HEAD_DIM = 64
ATT_GROUPS = ATT_HEADS // ATT_KV_HEADS
WINDOW = 128
ATT_BLOCK = 128
N_BUCKETS = 32
MAX_DISTANCE = 128
HG_HEADS = 4
HG_DK = 128
HG_DV = 128
HG_CHUNK = 64
D_FF = 2816
CONV_WIDTH = 3
EPS = 1e-6

ATT_Q = ATT_HEADS * ATT_HEAD_DIM
ATT_KV = ATT_KV_HEADS * ATT_HEAD_DIM
HG_K = HG_HEADS * HG_DK
HG_V = HG_HEADS * HG_DV
IN_SPLITS = (ATT_Q, ATT_KV, ATT_KV, HG_K, HG_K, HG_V, HG_V, D_MODEL, D_MODEL)
D_IN = ATT_Q + 2 * ATT_KV + 2 * HG_K + 2 * HG_V + 2 * D_MODEL

kernel_name = "hybrid_swa_hgrn2_gated_merge"


def rmsnorm(x, g):
    xf = x.astype(jnp.float32)
    y = xf * lax.rsqrt(jnp.mean(xf * xf, axis=-1, keepdims=True) + EPS)
    return (y * g.astype(jnp.float32)).astype(x.dtype)


def t5_bucket(dist):
    max_exact = N_BUCKETS // 2
    is_small = dist < max_exact
    d = jnp.maximum(dist, 1).astype(jnp.float32)
    large = max_exact + (jnp.log(d / max_exact) / math.log(MAX_DISTANCE / max_exact)
                         * (N_BUCKETS - max_exact)).astype(jnp.int32)
    large = jnp.minimum(large, N_BUCKETS - 1)
    return jnp.where(is_small, dist, large)


def swa_attention(q, k, v, sinks, rel_bias):
    B, S = q.shape[0], q.shape[1]
    nb = S // ATT_BLOCK
    qb = q.reshape(B, nb, ATT_BLOCK, ATT_KV_HEADS, ATT_GROUPS, ATT_HEAD_DIM)

    def band(t):
        tp = jnp.pad(t, ((0, 0), (ATT_BLOCK, 0), (0, 0), (0, 0)))
        prev = tp[:, :S].reshape(B, nb, ATT_BLOCK, ATT_KV_HEADS, ATT_HEAD_DIM)
        cur = t.reshape(B, nb, ATT_BLOCK, ATT_KV_HEADS, ATT_HEAD_DIM)
        return jnp.concatenate([prev, cur], axis=2)

    kb, vb = band(k), band(v)
    scale = ATT_HEAD_DIM ** -0.5
    scores = jnp.einsum('bnqhgd,bnkhd->bnhgqk', qb, kb).astype(jnp.float32) * scale

    qi = jnp.arange(ATT_BLOCK)[:, None] + ATT_BLOCK
    kj = jnp.arange(2 * ATT_BLOCK)[None, :]
    dist = qi - kj
    kpos = (jnp.arange(nb)[:, None] - 1) * ATT_BLOCK + jnp.arange(2 * ATT_BLOCK)[None, :]
    valid = ((dist >= 0) & (dist < WINDOW))[None] & (kpos >= 0)[:, None, :]
    bias = rel_bias.astype(jnp.float32)[t5_bucket(jnp.maximum(dist, 0))]
    bias = bias.transpose(2, 0, 1).reshape(ATT_KV_HEADS, ATT_GROUPS, ATT_BLOCK, 2 * ATT_BLOCK)
    scores = jnp.where(valid[None, :, None, None], scores + bias, -jnp.inf)

    sink = sinks.astype(jnp.float32).reshape(ATT_KV_HEADS, ATT_GROUPS, 1, 1)
    m = jnp.maximum(jnp.max(scores, axis=-1, keepdims=True), sink)
    p = jnp.exp(scores - m)
    denom = jnp.sum(p, axis=-1, keepdims=True) + jnp.exp(sink - m)
    probs = (p / denom).astype(v.dtype)
    out = jnp.einsum('bnhgqk,bnkhd->bnqhgd', probs, vb)
    return out.reshape(B, S, ATT_Q)


def hgrn2(q, f_pre, i, lb):
    B, S = q.shape[0], q.shape[1]
    nc = S // HG_CHUNK
    lbf = lb.astype(jnp.float32)
    f = lbf + (1.0 - lbf) * jax.nn.sigmoid(f_pre.astype(jnp.float32))
    logf = jnp.log(f)
    kk = 1.0 - f
    qq = jax.nn.silu(q.astype(jnp.float32))
    vv = i.astype(jnp.float32)

    def chunks(t):
        return t.reshape(B, nc, HG_CHUNK, HG_HEADS, t.shape[-1]).transpose(1, 0, 3, 2, 4)

    causal = jnp.tril(jnp.ones((HG_CHUNK, HG_CHUNK), dtype=bool))

    def step(state, inp):
        qc, kc, vc, gc = inp
        b = jnp.cumsum(gc, axis=2)
        diff = b[:, :, :, None, :] - b[:, :, None, :, :]
        decay = jnp.exp(jnp.where(causal[:, :, None], diff, -jnp.inf))
        att = jnp.einsum('bhtk,bhsk,bhtsk->bhts', qc, kc, decay)
        o = jnp.einsum('bhts,bhsv->bhtv', att, vc) + jnp.einsum('bhtk,bhkv->bhtv', qc * jnp.exp(b), state)
        b_last = b[:, :, -1:, :]
        state = jnp.exp(b_last[:, :, 0, :])[..., None] * state + \
            jnp.einsum('bhsk,bhsv->bhkv', kc * jnp.exp(b_last - b), vc)
        return state, o

    s0 = jnp.zeros((B, HG_HEADS, HG_DK, HG_DV), jnp.float32)
    _, o = lax.scan(step, s0, (chunks(qq), chunks(kk), chunks(vv), chunks(logf)))
    o = o.transpose(1, 0, 3, 2, 4).reshape(B, S, HG_HEADS, HG_DV)
    return o


def causal_dwconv(u, w, b):
    C = u.shape[-1]
    y = lax.conv_general_dilated(u, w[:, None, :].astype(u.dtype), window_strides=(1,),
                                 padding=[(CONV_WIDTH - 1, 0)],
                                 dimension_numbers=('NWC', 'WIO', 'NWC'),
                                 feature_group_count=C)
    return y + b.astype(u.dtype)


def setup_inputs(seed: int = 0) -> dict:
    key = jax.random.key(seed)
    ks = jax.random.split(key, 20)
    nrm = lambda k, shape, s: jax.random.normal(k, shape, jnp.float32) * s
    return {
        "x": nrm(ks[0], (BATCH, SEQ, D_MODEL), 1.0),
        "norm1": 1.0 + nrm(ks[1], (DEPTH, D_MODEL), 0.02),
        "w_in": nrm(ks[2], (DEPTH, D_MODEL, D_IN), D_MODEL ** -0.5),
        "q_norm": 1.0 + nrm(ks[3], (DEPTH, ATT_HEAD_DIM), 0.02),
        "k_norm": 1.0 + nrm(ks[4], (DEPTH, ATT_HEAD_DIM), 0.02),
        "sinks": nrm(ks[5], (DEPTH, ATT_HEADS), 0.5),
        "rel_bias": nrm(ks[6], (N_BUCKETS, ATT_HEADS), 0.1),
        "hg_lb": nrm(ks[7], (DEPTH, HG_K), 0.1),
        "hg_norm": 1.0 + nrm(ks[8], (DEPTH, HG_DV), 0.02),
        "w_pa": nrm(ks[9], (DEPTH, ATT_Q, D_MODEL), ATT_Q ** -0.5),
        "w_ph": nrm(ks[10], (DEPTH, HG_V, D_MODEL), HG_V ** -0.5),
        "w_out": nrm(ks[11], (DEPTH, D_MODEL, D_MODEL), D_MODEL ** -0.5),
        "norm2": 1.0 + nrm(ks[12], (DEPTH, D_MODEL), 0.02),
        "w_up": nrm(ks[13], (DEPTH, D_MODEL, 2 * D_FF), D_MODEL ** -0.5),
        "conv_w": nrm(ks[14], (DEPTH, CONV_WIDTH, 2 * D_FF), CONV_WIDTH ** -0.5),
        "conv_b": nrm(ks[15], (DEPTH, 2 * D_FF), 0.01),
        "w_down": nrm(ks[16], (DEPTH, D_FF, D_MODEL), D_FF ** -0.5),
    }


def reference(x, norm1, w_in, q_norm, k_norm, sinks, rel_bias, hg_lb, hg_norm,
              w_pa, w_ph, w_out, norm2, w_up, conv_w, conv_b, w_down):
    B, S = x.shape[0], x.shape[1]
    offs = tuple(int(o) for o in np.cumsum(IN_SPLITS)[:-1])
    p_lb = jax.nn.softmax(hg_lb.astype(jnp.float32), axis=0)
    lbs = jnp.cumsum(p_lb, axis=0) - p_lb[0:1]

    for l in range(DEPTH):
        h = rmsnorm(x, norm1[l])
        proj = h @ w_in[l].astype(h.dtype)
        q_a, k_a, v_a, q_r, f_r, i_r, g_r, gate_a, gate_r = jnp.split(proj, offs, axis=-1)

        q_a = rmsnorm(q_a.reshape(B, S, ATT_HEADS, ATT_HEAD_DIM), q_norm[l])
        k_a = rmsnorm(k_a.reshape(B, S, ATT_KV_HEADS, ATT_HEAD_DIM), k_norm[l])
        v_a = v_a.reshape(B, S, ATT_KV_HEADS, ATT_HEAD_DIM)
        a_out = swa_attention(q_a, k_a, v_a, sinks[l], rel_bias)

        r = hgrn2(q_r.reshape(B, S, HG_HEADS, HG_DK),
                  f_r.reshape(B, S, HG_HEADS, HG_DK),
                  i_r.reshape(B, S, HG_HEADS, HG_DV),
                  lbs[l].reshape(HG_HEADS, HG_DK))
        r = rmsnorm(r, hg_norm[l]).astype(x.dtype)
        r_out = (r * jax.nn.silu(g_r.reshape(B, S, HG_HEADS, HG_DV))).reshape(B, S, HG_V)

        merged = jax.nn.sigmoid(gate_a) * (a_out @ w_pa[l].astype(x.dtype)) + \
            jax.nn.sigmoid(gate_r) * (r_out @ w_ph[l].astype(x.dtype))
        x = x + merged @ w_out[l].astype(x.dtype)

        h2 = rmsnorm(x, norm2[l])
        u = causal_dwconv(h2 @ w_up[l].astype(x.dtype), conv_w[l], conv_b[l])
        u_gate, u_val = jnp.split(u, 2, axis=-1)
        x = x + (jax.nn.silu(u_gate) * u_val) @ w_down[l].astype(x.dtype)
    return x
```

```python
import functools
import math

import numpy as np
import jax
import jax.numpy as jnp
from jax import lax
from jax.experimental import pallas as pl
from jax.experimental.pallas import tpu as pltpu

F32 = jnp.float32
BF16 = jnp.bfloat16

D_MODEL = 1024
ATT_HEADS = 8
ATT_KV_HEADS = 2
ATT_HEAD_DIM = 64
WINDOW = 128
ATT_BLOCK = 128
N_BUCKETS = 32
MAX_DISTANCE = 128
HG_HEADS = 4
HG_DK = 128
HG_DV = 128
HG_CHUNK = 64
D_FF = 2816
CONV_WIDTH = 3
EPS = 1e-6

ATT_Q = ATT_HEADS * ATT_HEAD_DIM
ATT_KV = ATT_KV_HEADS * ATT_HEAD_DIM
HG_K = HG_HEADS * HG_DK
HG_V = HG_HEADS * HG_DV
IN_SPLITS = (ATT_Q, ATT_KV, ATT_KV, HG_K, HG_K, HG_V, HG_V, D_MODEL, D_MODEL)
IN_OFFS = tuple(int(o) for o in np.cumsum((0,) + IN_SPLITS))
D_IN = IN_OFFS[-1]

LANES = 128
VMEM_LIMIT_BYTES = 56 * 1024 * 1024

TM_PROJ = 512
TM_FFN = 512
FFN_CW = 256
TQ_ATT = 512
TB_HGRN = 1024
HG_LEVELS = (32, 16, 8, 4, 2, 1)


def _dot(a, b):
    return jnp.dot(a, b, preferred_element_type=F32)


def _dot_nt(a, b):
    return lax.dot_general(a, b, (((1,), (1,)), ((), ())), preferred_element_type=F32)


def _dot_tn(a, b):
    return lax.dot_general(a, b, (((0,), (0,)), ((), ())), preferred_element_type=F32)


def _sigmoid(x):
    return 1.0 / (1.0 + jnp.exp(-x))


def _rms_scale(x):
    return lax.rsqrt(jnp.mean(x * x, axis=-1, keepdims=True) + EPS)


def _const_spec(shape):
    nd = len(shape)
    return pl.BlockSpec(shape, lambda *_: (0,) * nd)


def _params():
    return pltpu.CompilerParams(vmem_limit_bytes=VMEM_LIMIT_BYTES)


def _inproj_kernel(x_ref, n1_ref, w_ref, qg_ref, kg_ref, bdq_ref, bdk_ref,
                   qa_ref, ka_ref, kb_ref, va_ref, vb_ref,
                   qr_ref, fr_ref, ir_ref, gr_ref, sga_ref, sgr_ref):
    x = x_ref[...]
    h = (x * _rms_scale(x) * n1_ref[...]).astype(BF16)

    def proj(seg):
        return _dot(h, w_ref[:, IN_OFFS[seg]:IN_OFFS[seg + 1]])

    q = proj(0)
    q_ms = _dot((q * q).astype(BF16), bdq_ref[...]) * (1.0 / ATT_HEAD_DIM)
    qa_ref[...] = (q * lax.rsqrt(q_ms + EPS) * qg_ref[...]).astype(BF16)

    k = proj(1)
    k_ms = _dot((k * k).astype(BF16), bdk_ref[...]) * (1.0 / ATT_HEAD_DIM)
    kn = k * lax.rsqrt(k_ms + EPS) * kg_ref[...]
    ka_ref[...] = kn.astype(BF16)
    kb_ref[...] = pltpu.roll(kn, ATT_HEAD_DIM, 1).astype(BF16)

    v = proj(2)
    va_ref[...] = v.astype(BF16)
    vb_ref[...] = pltpu.roll(v, ATT_HEAD_DIM, 1).astype(BF16)

    qr_ref[...] = proj(3).astype(BF16)
    fr_ref[...] = proj(4)
    ir_ref[...] = proj(5).astype(BF16)
    gr_ref[...] = proj(6).astype(BF16)
    sga_ref[...] = _sigmoid(proj(7)).astype(BF16)
    sgr_ref[...] = _sigmoid(proj(8)).astype(BF16)


def _inproj(xf, n1, w_in, qg, kg, bdq, bdk):
    T = xf.shape[0]
    tm = TM_PROJ
    row = lambda n: pl.BlockSpec((tm, n), lambda i: (i, 0))
    out_dims = [(ATT_Q, BF16), (ATT_KV, BF16), (ATT_KV, BF16), (ATT_KV, BF16), (ATT_KV, BF16),
                (HG_K, BF16), (HG_K, F32), (HG_V, BF16), (HG_V, BF16),
                (D_MODEL, BF16), (D_MODEL, BF16)]
    return pl.pallas_call(
        _inproj_kernel,
        grid=(T // tm,),
        in_specs=[row(D_MODEL), _const_spec((1, D_MODEL)), _const_spec((D_MODEL, D_IN)),
                  _const_spec((1, ATT_Q)), _const_spec((1, ATT_KV)),
                  _const_spec((ATT_Q, ATT_Q)), _const_spec((ATT_KV, ATT_KV))],
        out_specs=[row(n) for n, _ in out_dims],
        out_shape=[jax.ShapeDtypeStruct((T, n), dt) for n, dt in out_dims],
        compiler_params=_params(),
        name="inproj",
    )(xf, n1, w_in, qg, kg, bdq, bdk)


def _attn_kernel(sink_ref, q_ref, kac_ref, kbc_ref, vac_ref, vbc_ref,
                 kap_ref, kbp_ref, vap_ref, vbp_ref, bias_ref, o_ref,
                 ka_s, kb_s, va_s, vb_s):
    blk = ATT_BLOCK
    n_qb = q_ref.shape[0] // blk
    seq_blk = pl.program_id(1)

    for s_ref, p_ref, c_ref in ((ka_s, kap_ref, kac_ref), (kb_s, kbp_ref, kbc_ref),
                                (va_s, vap_ref, vac_ref), (vb_s, vbp_ref, vbc_ref)):
        s_ref[0:blk, :] = p_ref[...]
        s_ref[blk:, :] = c_ref[...]

    lane_q = lax.broadcasted_iota(jnp.int32, (blk, LANES), 1)
    lane_kv = lax.broadcasted_iota(jnp.int32, (2 * blk, LANES), 1)
    col = lax.broadcasted_iota(jnp.int32, (blk, 2 * blk), 1)
    lo_q = lane_q < ATT_HEAD_DIM
    q_lo = jnp.where(lo_q, 1.0, 0.0).astype(BF16)
    q_hi = jnp.where(lo_q, 0.0, 1.0).astype(BF16)
    ind_lo = jnp.where(lane_kv < ATT_HEAD_DIM, 1.0, 0.0).astype(BF16)
    ind_hi = jnp.where(lane_kv < ATT_HEAD_DIM, 0.0, 1.0).astype(BF16)
    no_prev = jnp.where(seq_blk == 0, -jnp.inf, 0.0).astype(F32)

    for qb in range(n_qb):
        r0 = qb * blk
        band = {"a": (ka_s[r0:r0 + 2 * blk, :], va_s[r0:r0 + 2 * blk, :]),
                "b": (kb_s[r0:r0 + 2 * blk, :], vb_s[r0:r0 + 2 * blk, :])}
        for g in range(ATT_HEADS // 2):
            even_src, odd_src = ("a", "b") if g < 2 else ("b", "a")
            q2 = q_ref[r0:r0 + blk, g * LANES:(g + 1) * LANES]
            probs, sink_terms = [], []
            for half, src in ((0, even_src), (1, odd_src)):
                head = 2 * g + half
                qh = q2 * (q_lo if half == 0 else q_hi)
                s = _dot_nt(qh, band[src][0]) + bias_ref[head]
                if qb == 0:
                    s = s + jnp.where(col < blk, no_prev, 0.0)
                sink = sink_ref[head]
                m = jnp.maximum(jnp.max(s, axis=-1, keepdims=True), sink)
                probs.append(jnp.exp(s - m).astype(BF16))
                sink_terms.append(jnp.exp(sink - m))
            p_cat = jnp.concatenate(probs, axis=1)
            v_even = band[even_src][1] * ind_lo
            v_odd = band[odd_src][1] * ind_hi
            v_stack = jnp.concatenate(
                [jnp.concatenate([v_even, ind_lo], axis=1),
                 jnp.concatenate([v_odd, ind_hi], axis=1)], axis=0)
            res = _dot(p_cat, v_stack)
            den = res[:, LANES:] + jnp.where(lo_q, sink_terms[0], sink_terms[1])
            o_ref[r0:r0 + blk, g * LANES:(g + 1) * LANES] = (res[:, :LANES] / den).astype(BF16)


def _attn(sinks, qa, ka, kb, va, vb, bias_tab, B, S):
    T = qa.shape[0]
    tq = TQ_ATT
    blk = ATT_BLOCK
    per_seq = S // tq
    ratio = tq // blk
    cur = lambda n: pl.BlockSpec((tq, n), lambda b, i: (b * per_seq + i, 0))
    prev = pl.BlockSpec(
        (blk, ATT_KV), lambda b, i: (b * (S // blk) + jnp.maximum(i * ratio - 1, 0), 0))
    return pl.pallas_call(
        _attn_kernel,
        grid=(B, per_seq),
        in_specs=[pl.BlockSpec(memory_space=pltpu.SMEM), cur(ATT_Q),
                  cur(ATT_KV), cur(ATT_KV), cur(ATT_KV), cur(ATT_KV),
                  prev, prev, prev, prev,
                  _const_spec((ATT_HEADS, blk, 2 * blk))],
        out_specs=cur(ATT_Q),
        out_shape=jax.ShapeDtypeStruct((T, ATT_Q), BF16),
        scratch_shapes=[pltpu.VMEM((tq + blk, ATT_KV), BF16)] * 4,
        compiler_params=_params(),
        name="swa_attn",
    )(sinks, qa, ka, kb, va, vb, ka, kb, va, vb, bias_tab)


def _hgrn_consts():
    C = HG_CHUNK
    t = np.arange(C)[:, None]
    u = np.arange(C)[None, :]
    rows = [u <= t, u > t]
    masks = []
    for m in HG_LEVELS:
        blk_t = (t // (2 * m)) * (2 * m)
        later = t >= blk_t + m
        rows.append(later & (u >= blk_t + m) & (u <= t))
        rows.append((~later) & (u > t) & (u <= blk_t + m - 1))
        blk_s = (u // (2 * m)) * (2 * m)
        masks.append(later & (u < blk_s + m) & (blk_s == blk_t))
    masks.append(t == u)
    return (np.concatenate(rows, axis=0).astype(np.float32),
            np.stack(masks, axis=0).astype(np.float32))


def _hgrn_kernel(qr_ref, fr_ref, ir_ref, gr_ref, lb_ref, hn_ref, sel_ref, mask_ref,
                 o_ref, st_ref):
    C = HG_CHUNK
    n_chunks = qr_ref.shape[0] // C

    @pl.when(pl.program_id(1) == 0)
    def _():
        st_ref[...] = jnp.zeros_like(st_ref)

    lb = lb_ref[...]
    hn = hn_ref[...]
    sel = sel_ref[...]

    def chunk(c, carry):
        r0 = pl.multiple_of(c * C, C)
        rows = pl.ds(r0, C)
        f = lb + (1.0 - lb) * _sigmoid(fr_ref[rows, :])
        g = jnp.log(f)
        kk = 1.0 - f
        qv = qr_ref[rows, :].astype(F32)
        qq = qv * _sigmoid(qv)
        vv = ir_ref[rows, :]
        gv = gr_ref[rows, :].astype(F32)
        gate = gv * _sigmoid(gv)

        g1 = g.astype(BF16)
        rem = g - g1.astype(F32)
        g2 = rem.astype(BF16)
        g3 = (rem - g2.astype(F32)).astype(BF16)
        ex = jnp.exp(_dot(sel, g1) + _dot(sel, g2) + _dot(sel, g3))

        def blk(i):
            return ex[i * C:(i + 1) * C, :]

        q_state = (qq * blk(0)).astype(BF16)
        k_state = (kk * blk(1)).astype(BF16)
        q_lvl = [(qq * blk(2 + 2 * i)).astype(BF16) for i in range(len(HG_LEVELS))]
        k_lvl = [(kk * blk(3 + 2 * i)).astype(BF16) for i in range(len(HG_LEVELS) - 1)]
        q_lvl.append(qq.astype(BF16))
        k_lvl += [kk.astype(BF16)] * 2

        for h in range(HG_HEADS):
            sl = slice(h * HG_DK, (h + 1) * HG_DK)
            att = jnp.zeros((C, C), F32)
            for i in range(len(q_lvl)):
                att = att + _dot_nt(q_lvl[i][:, sl], k_lvl[i][:, sl]) * mask_ref[i]
            st = st_ref[h]
            v_h = vv[:, sl]
            o = _dot(att.astype(BF16), v_h) + _dot_nt(q_state[:, sl], st.astype(BF16))
            st_ref[h] = st * ex[C - 1:C, sl] + _dot_tn(v_h, k_state[:, sl])
            r = o * _rms_scale(o) * hn
            o_ref[rows, sl] = (r * gate[:, sl]).astype(BF16)
        return carry

    lax.fori_loop(0, n_chunks, chunk, 0)


def _hgrn(qr, fr, ir, gr, lb, hn, sel, masks, B, S):
    T = qr.shape[0]
    tb = TB_HGRN
    per_seq = S // tb
    row = pl.BlockSpec((tb, HG_K), lambda b, i: (b * per_seq + i, 0))
    return pl.pallas_call(
        _hgrn_kernel,
        grid=(B, per_seq),
        in_specs=[row, row, row, row, _const_spec((1, HG_K)), _const_spec((1, HG_DV)),
                  _const_spec(sel.shape), _const_spec(masks.shape)],
        out_specs=row,
        out_shape=jax.ShapeDtypeStruct((T, HG_V), BF16),
        scratch_shapes=[pltpu.VMEM((HG_HEADS, HG_DV, HG_DK), F32)],
        compiler_params=_params(),
        name="hgrn2",
    )(qr, fr, ir, gr, lb, hn, sel, masks)


def _merge_kernel(a_ref, r_ref, sga_ref, sgr_ref, x_ref, wpa_ref, wph_ref, wout_ref, n2_ref,
                  x1_ref, h2_ref):
    pa = _dot(a_ref[...], wpa_ref[...])
    ph = _dot(r_ref[...], wph_ref[...])
    merged = sga_ref[...].astype(F32) * pa + sgr_ref[...].astype(F32) * ph
    x1 = x_ref[...] + _dot(merged.astype(BF16), wout_ref[...])
    x1_ref[...] = x1
    h2_ref[...] = (x1 * _rms_scale(x1) * n2_ref[...]).astype(BF16)


def _merge(a, r, sga, sgr, xf, wpa, wph, wout, n2):
    T = xf.shape[0]
    tm = TM_PROJ
    row = lambda n: pl.BlockSpec((tm, n), lambda i: (i, 0))
    return pl.pallas_call(
        _merge_kernel,
        grid=(T // tm,),
        in_specs=[row(ATT_Q), row(HG_V), row(D_MODEL), row(D_MODEL), row(D_MODEL),
                  _const_spec((ATT_Q, D_MODEL)), _const_spec((HG_V, D_MODEL)),
                  _const_spec((D_MODEL, D_MODEL)), _const_spec((1, D_MODEL))],
        out_specs=[row(D_MODEL), row(D_MODEL)],
        out_shape=[jax.ShapeDtypeStruct((T, D_MODEL), F32),
                   jax.ShapeDtypeStruct((T, D_MODEL), BF16)],
        compiler_params=_params(),
        name="merge_out",
    )(a, r, sga, sgr, xf, wpa, wph, wout, n2)


def _ffn_kernel(h2_ref, x1_ref, wup_ref, cw_ref, cb_ref, wdn_ref, o_ref,
                carry_ref, work_ref, *, tiles_per_seq):
    tm = h2_ref.shape[0]
    halo = carry_ref.shape[0]

    @pl.when(pl.program_id(0) % tiles_per_seq == 0)
    def _():
        carry_ref[...] = jnp.zeros_like(carry_ref)

    h2 = h2_ref[...]

    def conv(slot, off):
        cols = slice(off, off + FFN_CW)
        u = _dot(h2, wup_ref[:, cols])
        work_ref[slot, 0:halo, :] = carry_ref[:, cols]
        work_ref[slot, halo:halo + tm, :] = u
        carry_ref[:, cols] = u[tm - halo:tm, :]
        w = cw_ref[:, cols]
        return (w[0:1, :] * work_ref[slot, halo - 2:halo - 2 + tm, :]
                + w[1:2, :] * work_ref[slot, halo - 1:halo - 1 + tm, :]
                + w[2:3, :] * u + cb_ref[:, cols])

    acc = x1_ref[...]
    for c in range(D_FF // FFN_CW):
        ug = conv(0, c * FFN_CW)
        uv = conv(1, D_FF + c * FFN_CW)
        act = (ug * _sigmoid(ug) * uv).astype(BF16)
        acc = acc + _dot(act, wdn_ref[c * FFN_CW:(c + 1) * FFN_CW, :])
    o_ref[...] = acc


def _ffn(h2, x1, wup, cw, cb, wdn, S):
    T = h2.shape[0]
    tm = TM_FFN
    halo = 8
    row = pl.BlockSpec((tm, D_MODEL), lambda i: (i, 0))
    return pl.pallas_call(
        functools.partial(_ffn_kernel, tiles_per_seq=S // tm),
        grid=(T // tm,),
        in_specs=[row, row, _const_spec((D_MODEL, 2 * D_FF)),
                  _const_spec((CONV_WIDTH, 2 * D_FF)), _const_spec((1, 2 * D_FF)),
                  _const_spec((D_FF, D_MODEL))],
        out_specs=row,
        out_shape=jax.ShapeDtypeStruct((T, D_MODEL), F32),
        scratch_shapes=[pltpu.VMEM((halo, 2 * D_FF), F32),
                        pltpu.VMEM((2, halo + tm, FFN_CW), F32)],
        compiler_params=_params(),
        name="conv_ffn",
    )(h2, x1, wup, cw, cb, wdn)


def _t5_bucket_table():
    i = np.arange(ATT_BLOCK)[:, None]
    j = np.arange(2 * ATT_BLOCK)[None, :]
    dist = i + ATT_BLOCK - j
    valid = (dist >= 0) & (dist < WINDOW)
    d = np.maximum(dist, 0)
    max_exact = N_BUCKETS // 2
    large = max_exact + (np.log(np.maximum(d, 1) / max_exact) / math.log(MAX_DISTANCE / max_exact)
                         * (N_BUCKETS - max_exact)).astype(np.int32)
    large = np.minimum(large, N_BUCKETS - 1)
    return np.where(d < max_exact, d, large), valid


def _block_diag_ones(n, blk):
    idx = np.arange(n) // blk
    return jnp.asarray((idx[:, None] == idx[None, :]).astype(np.float32), BF16)


def kernel(x, norm1, w_in, q_norm, k_norm, sinks, rel_bias, hg_lb, hg_norm,
           w_pa, w_ph, w_out, norm2, w_up, conv_w, conv_b, w_down):
    B, S, D = x.shape
    T = B * S
    depth = w_in.shape[0]
    assert D == D_MODEL and T % TM_PROJ == 0 and S % TM_FFN == 0
    assert S % TQ_ATT == 0 and S % TB_HGRN == 0

    p_lb = jax.nn.softmax(hg_lb.astype(F32), axis=0)
    lbs = jnp.cumsum(p_lb, axis=0) - p_lb[0:1]

    bucket, valid = _t5_bucket_table()
    bias = rel_bias.astype(F32)[bucket]
    bias_tab = jnp.where(valid[None], bias.transpose(2, 0, 1), -jnp.inf)

    bdq = _block_diag_ones(ATT_Q, ATT_HEAD_DIM)
    bdk = _block_diag_ones(ATT_KV, ATT_HEAD_DIM)
    sel_np, masks_np = _hgrn_consts()
    sel = jnp.asarray(sel_np, BF16)
    masks = jnp.asarray(masks_np, F32)
    scale = ATT_HEAD_DIM ** -0.5

    xf = x.reshape(T, D)
    for l in range(depth):
        qg = (jnp.tile(q_norm[l].astype(F32), ATT_HEADS) * scale).reshape(1, ATT_Q)
        kg = jnp.tile(k_norm[l].astype(F32), ATT_KV_HEADS).reshape(1, ATT_KV)
        (qa, ka, kb, va, vb, qr, fr, ir, gr, sga, sgr) = _inproj(
            xf, norm1[l].reshape(1, D), w_in[l].astype(BF16), qg, kg, bdq, bdk)
        a = _attn(sinks[l].astype(F32), qa, ka, kb, va, vb, bias_tab, B, S)
        r = _hgrn(qr, fr, ir, gr, lbs[l].reshape(1, HG_K), hg_norm[l].reshape(1, HG_DV).astype(F32),
                  sel, masks, B, S)
        x1, h2 = _merge(a, r, sga, sgr, xf, w_pa[l].astype(BF16), w_ph[l].astype(BF16),
                        w_out[l].astype(BF16), norm2[l].reshape(1, D))
        xf = _ffn(h2, x1, w_up[l].astype(BF16), conv_w[l].astype(F32),
                  conv_b[l].reshape(1, 2 * D_FF).astype(F32), w_down[l].astype(BF16), S)
    return xf.reshape(B, S, D)
```

```python
import functools
import math

import numpy as np
import jax
import jax.numpy as jnp
from jax import lax
from jax.experimental import pallas as pl
from jax.experimental.pallas import tpu as pltpu

F32 = jnp.float32
BF16 = jnp.bfloat16

D_MODEL = 1024
ATT_HEADS = 8
ATT_KV_HEADS = 2
ATT_HEAD_DIM = 64
WINDOW = 128
ATT_BLOCK = 128
N_BUCKETS = 32
MAX_DISTANCE = 128
HG_HEADS = 4
HG_DK = 128
HG_DV = 128
HG_CHUNK = 64
D_FF = 2816
CONV_WIDTH = 3
EPS = 1e-6

ATT_Q = ATT_HEADS * ATT_HEAD_DIM
ATT_KV = ATT_KV_HEADS * ATT_HEAD_DIM
HG_K = HG_HEADS * HG_DK
HG_V = HG_HEADS * HG_DV
IN_SPLITS = (ATT_Q, ATT_KV, ATT_KV, HG_K, HG_K, HG_V, HG_V, D_MODEL, D_MODEL)
IN_OFFS = tuple(int(o) for o in np.cumsum((0,) + IN_SPLITS))
D_IN = IN_OFFS[-1]

LANES = 128
VMEM_LIMIT_BYTES = 56 * 1024 * 1024

TM_PROJ = 512
TM_FFN = 512
FFN_CW = 256
TQ_ATT = 512
TB_HGRN = 1024
HG_BLOCK = 256
HG_COARSE = (128, 64)
HG_LEVELS = (32, 16, 8, 4, 2, 1)
HG_SAFE_SPAN = 60.0


def _dot(a, b):
    return jnp.dot(a, b, preferred_element_type=F32)


def _dot_nt(a, b):
    return lax.dot_general(a, b, (((1,), (1,)), ((), ())), preferred_element_type=F32)


def _dot_tn(a, b):
    return lax.dot_general(a, b, (((0,), (0,)), ((), ())), preferred_element_type=F32)


def _sigmoid(x):
    return 1.0 / (1.0 + jnp.exp(-x))


def _rms_scale(x):
    return lax.rsqrt(jnp.mean(x * x, axis=-1, keepdims=True) + EPS)


def _const_spec(shape):
    nd = len(shape)
    return pl.BlockSpec(shape, lambda *_: (0,) * nd)


def _params():
    return pltpu.CompilerParams(vmem_limit_bytes=VMEM_LIMIT_BYTES)


def _inproj_kernel(x_ref, n1_ref, w_ref, qg_ref, kg_ref, bdq_ref, bdk_ref, lb_ref,
                   qa_ref, ka_ref, kb_ref, va_ref, vb_ref,
                   hq_ref, hk_ref, ghi_ref, glo_ref, hv_ref, hgate_ref, sga_ref, sgr_ref):
    x = x_ref[...]
    h = (x * _rms_scale(x) * n1_ref[...]).astype(BF16)

    def proj(seg):
        return _dot(h, w_ref[:, IN_OFFS[seg]:IN_OFFS[seg + 1]])

    q = proj(0)
    q_ms = _dot((q * q).astype(BF16), bdq_ref[...]) * (1.0 / ATT_HEAD_DIM)
    qa_ref[...] = (q * lax.rsqrt(q_ms + EPS) * qg_ref[...]).astype(BF16)

    k = proj(1)
    k_ms = _dot((k * k).astype(BF16), bdk_ref[...]) * (1.0 / ATT_HEAD_DIM)
    kn = k * lax.rsqrt(k_ms + EPS) * kg_ref[...]
    ka_ref[...] = kn.astype(BF16)
    kb_ref[...] = pltpu.roll(kn, ATT_HEAD_DIM, 1).astype(BF16)

    v = proj(2)
    va_ref[...] = v.astype(BF16)
    vb_ref[...] = pltpu.roll(v, ATT_HEAD_DIM, 1).astype(BF16)

    qr = proj(3)
    hq_ref[...] = (qr * _sigmoid(qr)).astype(BF16)
    lb = lb_ref[...]
    f = lb + (1.0 - lb) * _sigmoid(proj(4))
    hk_ref[...] = (1.0 - f).astype(BF16)
    g = jnp.log(f)
    g_hi = g.astype(BF16)
    ghi_ref[...] = g_hi
    glo_ref[...] = (g - g_hi.astype(F32)).astype(BF16)
    hv_ref[...] = proj(5).astype(BF16)
    gr = proj(6)
    hgate_ref[...] = (gr * _sigmoid(gr)).astype(BF16)
    sga_ref[...] = _sigmoid(proj(7)).astype(BF16)
    sgr_ref[...] = _sigmoid(proj(8)).astype(BF16)


def _inproj(xf, n1, w_in, qg, kg, bdq, bdk, lb):
    T = xf.shape[0]
    tm = TM_PROJ
    row = lambda n: pl.BlockSpec((tm, n), lambda i: (i, 0))
    out_widths = [ATT_Q, ATT_KV, ATT_KV, ATT_KV, ATT_KV,
                  HG_K, HG_K, HG_K, HG_K, HG_V, HG_V, D_MODEL, D_MODEL]
    return pl.pallas_call(
        _inproj_kernel,
        grid=(T // tm,),
        in_specs=[row(D_MODEL), _const_spec((1, D_MODEL)), _const_spec((D_MODEL, D_IN)),
                  _const_spec((1, ATT_Q)), _const_spec((1, ATT_KV)),
                  _const_spec((ATT_Q, ATT_Q)), _const_spec((ATT_KV, ATT_KV)),
                  _const_spec((1, HG_K))],
        out_specs=[row(n) for n in out_widths],
        out_shape=[jax.ShapeDtypeStruct((T, n), BF16) for n in out_widths],
        compiler_params=_params(),
        name="inproj",
    )(xf, n1, w_in, qg, kg, bdq, bdk, lb)


def _attn_kernel(sink_ref, q_ref, kac_ref, kbc_ref, vac_ref, vbc_ref,
                 kap_ref, kbp_ref, vap_ref, vbp_ref, bias_ref, o_ref,
                 ka_s, kb_s, va_s, vb_s):
    blk = ATT_BLOCK
    n_qb = q_ref.shape[0] // blk
    seq_blk = pl.program_id(1)

    for s_ref, p_ref, c_ref in ((ka_s, kap_ref, kac_ref), (kb_s, kbp_ref, kbc_ref),
                                (va_s, vap_ref, vac_ref), (vb_s, vbp_ref, vbc_ref)):
        s_ref[0:blk, :] = p_ref[...]
        s_ref[blk:, :] = c_ref[...]

    lane_q = lax.broadcasted_iota(jnp.int32, (blk, LANES), 1)
    lane_kv = lax.broadcasted_iota(jnp.int32, (2 * blk, LANES), 1)
    col = lax.broadcasted_iota(jnp.int32, (blk, 2 * blk), 1)
    lo_q = lane_q < ATT_HEAD_DIM
    q_lo = jnp.where(lo_q, 1.0, 0.0).astype(BF16)
    q_hi = jnp.where(lo_q, 0.0, 1.0).astype(BF16)
    ind_lo = jnp.where(lane_kv < ATT_HEAD_DIM, 1.0, 0.0).astype(BF16)
    ind_hi = jnp.where(lane_kv < ATT_HEAD_DIM, 0.0, 1.0).astype(BF16)
    no_prev = jnp.where(seq_blk == 0, -jnp.inf, 0.0).astype(F32)

    for qb in range(n_qb):
        r0 = qb * blk
        band = {"a": (ka_s[r0:r0 + 2 * blk, :], va_s[r0:r0 + 2 * blk, :]),
                "b": (kb_s[r0:r0 + 2 * blk, :], vb_s[r0:r0 + 2 * blk, :])}
        for g in range(ATT_HEADS // 2):
            even_src, odd_src = ("a", "b") if g < 2 else ("b", "a")
            q2 = q_ref[r0:r0 + blk, g * LANES:(g + 1) * LANES]
            probs, sink_terms = [], []
            for half, src in ((0, even_src), (1, odd_src)):
                head = 2 * g + half
                qh = q2 * (q_lo if half == 0 else q_hi)
                s = _dot_nt(qh, band[src][0]) + bias_ref[head]
                if qb == 0:
                    s = s + jnp.where(col < blk, no_prev, 0.0)
                sink = sink_ref[head]
                m = jnp.maximum(jnp.max(s, axis=-1, keepdims=True), sink)
                probs.append(jnp.exp(s - m).astype(BF16))
                sink_terms.append(jnp.exp(sink - m))
            p_cat = jnp.concatenate(probs, axis=1)
            v_even = band[even_src][1] * ind_lo
            v_odd = band[odd_src][1] * ind_hi
            v_stack = jnp.concatenate(
                [jnp.concatenate([v_even, ind_lo], axis=1),
                 jnp.concatenate([v_odd, ind_hi], axis=1)], axis=0)
            res = _dot(p_cat, v_stack)
            den = res[:, LANES:] + jnp.where(lo_q, sink_terms[0], sink_terms[1])
            o_ref[r0:r0 + blk, g * LANES:(g + 1) * LANES] = (res[:, :LANES] / den).astype(BF16)


def _attn(sinks, qa, ka, kb, va, vb, bias_tab, B, S):
    T = qa.shape[0]
    tq = TQ_ATT
    blk = ATT_BLOCK
    per_seq = S // tq
    ratio = tq // blk
    cur = lambda n: pl.BlockSpec((tq, n), lambda b, i: (b * per_seq + i, 0))
    prev = pl.BlockSpec(
        (blk, ATT_KV), lambda b, i: (b * (S // blk) + jnp.maximum(i * ratio - 1, 0), 0))
    return pl.pallas_call(
        _attn_kernel,
        grid=(B, per_seq),
        in_specs=[pl.BlockSpec(memory_space=pltpu.SMEM), cur(ATT_Q),
                  cur(ATT_KV), cur(ATT_KV), cur(ATT_KV), cur(ATT_KV),
                  prev, prev, prev, prev,
                  _const_spec((ATT_HEADS, blk, 2 * blk))],
        out_specs=cur(ATT_Q),
        out_shape=jax.ShapeDtypeStruct((T, ATT_Q), BF16),
        scratch_shapes=[pltpu.VMEM((tq + blk, ATT_KV), BF16)] * 4,
        compiler_params=_params(),
        name="swa_attn",
    )(sinks, qa, ka, kb, va, vb, ka, kb, va, vb, bias_tab)


def _hgrn_masks():
    n = HG_BLOCK
    t = np.arange(n)[:, None]
    s = np.arange(n)[None, :]

    def level(m):
        return ((t // (2 * m)) == (s // (2 * m))) & ((t // m) % 2 == 1) & ((s // m) % 2 == 0)

    masks = [level(m) for m in HG_COARSE]
    masks.append(((t // HG_CHUNK) == (s // HG_CHUNK)) & (s <= t))
    masks += [level(m) for m in HG_LEVELS] + [t == s]
    return np.stack(masks, axis=0).astype(np.float32), (s <= t).astype(np.float32)


def _hgrn_kernel(q_ref, k_ref, ghi_ref, glo_ref, v_ref, gate_ref, hn_ref, tri_ref, mask_ref,
                 o_ref, st_ref):
    N = HG_BLOCK
    W = q_ref.shape[1]
    n_blocks = q_ref.shape[0] // N
    heads = [slice(h * HG_DK, (h + 1) * HG_DK) for h in range(HG_HEADS)]
    n_coarse = len(HG_COARSE)

    @pl.when(pl.program_id(1) == 0)
    def _():
        st_ref[...] = jnp.zeros_like(st_ref)

    hn = hn_ref[...]
    tri = tri_ref[...]

    def block(c, carry):
        r0 = pl.multiple_of(c * N, N)
        rows = pl.ds(r0, N)
        qq = q_ref[rows, :]
        kk = k_ref[rows, :]
        vv = v_ref[rows, :]
        g_hi = ghi_ref[rows, :]
        g_lo = glo_ref[rows, :]

        b = _dot(tri, g_hi) + _dot(tri, g_lo)
        b_last = b[N - 1:N, :]
        decay = jnp.exp(b_last)

        def scaled(e):
            eb = e.astype(BF16)
            return qq * eb, kk * eb

        def offset(m):
            ref = [jnp.broadcast_to(b[k + m - 1:k + m, :], (2 * m, W)) for k in range(0, N, 2 * m)]
            return b - (ref[0] if len(ref) == 1 else jnp.concatenate(ref, axis=0))

        def finish(h, att_fine, att_coarse, o_state):
            sl = heads[h]
            o = _dot((att_fine + att_coarse).astype(BF16), vv[:, sl]) + o_state
            r = o * _rms_scale(o) * hn
            o_ref[rows, sl] = (r * gate_ref[rows, sl].astype(F32)).astype(BF16)

        q_state = qq * jnp.exp(b).astype(BF16)
        k_state = kk * jnp.exp(b_last - b).astype(BF16)
        coarse = [scaled(jnp.exp(-jnp.abs(offset(m)))) for m in HG_COARSE]

        d_mid = offset(HG_CHUNK // 2)
        span = jnp.max(jnp.abs(d_mid))
        e_mid = jnp.exp(d_mid)
        q_mid = qq * e_mid.astype(BF16)
        k_mid = kk * (1.0 / e_mid).astype(BF16)
        in_chunk = mask_ref[n_coarse] > 0.5

        att_coarse, o_state = [], []
        for h, sl in enumerate(heads):
            acc = _dot_nt(coarse[0][0][:, sl], coarse[0][1][:, sl]) * mask_ref[0]
            for i in range(1, n_coarse):
                acc = acc + _dot_nt(coarse[i][0][:, sl], coarse[i][1][:, sl]) * mask_ref[i]
            att_coarse.append(acc)
            st = st_ref[h]
            o_state.append(_dot_nt(q_state[:, sl], st.astype(BF16)))
            st_ref[h] = st * decay[:, sl] + _dot_tn(vv[:, sl], k_state[:, sl])
            fine = jnp.where(in_chunk, _dot_nt(q_mid[:, sl], k_mid[:, sl]), 0.0)
            finish(h, fine, acc, o_state[h])

        @pl.when(span >= HG_SAFE_SPAN)
        def _():
            f = jnp.exp(g_hi.astype(F32) + g_lo.astype(F32))
            scales = [jnp.exp(-jnp.abs(d_mid))]
            scales += [jnp.exp(-jnp.abs(offset(m))) for m in HG_LEVELS[1:] if 2 * m >= 8]
            row = lax.broadcasted_iota(jnp.int32, (N, W), 0)
            f_next = pltpu.roll(f, N - 1, 0)
            f_prev = pltpu.roll(f, 1, 0)
            r4 = row & 3
            scales.append(jnp.where(r4 == 0, f_next,
                                    jnp.where(r4 == 1, 1.0, jnp.where(r4 == 2, f, f * f_prev))))
            scales.append(jnp.where((row & 1) == 1, f, 1.0))
            pairs = [scaled(e) for e in scales] + [(qq, kk)]
            for h, sl in enumerate(heads):
                fine = jnp.zeros((N, N), F32)
                for i, (ql, kl) in enumerate(pairs):
                    fine = fine + _dot_nt(ql[:, sl], kl[:, sl]) * mask_ref[n_coarse + 1 + i]
                finish(h, fine, att_coarse[h], o_state[h])

        return carry

    lax.fori_loop(0, n_blocks, block, 0)


def _hgrn(hq, hk, ghi, glo, hv, hgate, hn, B, S):
    T = hq.shape[0]
    masks_np, tri_np = _hgrn_masks()
    masks = jnp.asarray(masks_np, F32)
    tri = jnp.asarray(tri_np, BF16)
    tb = TB_HGRN
    per_seq = S // tb
    row = pl.BlockSpec((tb, HG_K), lambda b, i: (b * per_seq + i, 0))
    return pl.pallas_call(
        _hgrn_kernel,
        grid=(B, per_seq),
        in_specs=[row, row, row, row, row, row, _const_spec((1, HG_DV)),
                  _const_spec(tri.shape), _const_spec(masks.shape)],
        out_specs=row,
        out_shape=jax.ShapeDtypeStruct((T, HG_V), BF16),
        scratch_shapes=[pltpu.VMEM((HG_HEADS, HG_DV, HG_DK), F32)],
        compiler_params=_params(),
        name="hgrn2",
    )(hq, hk, ghi, glo, hv, hgate, hn, tri, masks)


def _merge_kernel(a_ref, r_ref, sga_ref, sgr_ref, x_ref, wpa_ref, wph_ref, wout_ref, n2_ref,
                  x1_ref, h2_ref):
    pa = _dot(a_ref[...], wpa_ref[...])
    ph = _dot(r_ref[...], wph_ref[...])
    merged = sga_ref[...].astype(F32) * pa + sgr_ref[...].astype(F32) * ph
    x1 = x_ref[...] + _dot(merged.astype(BF16), wout_ref[...])
    x1_ref[...] = x1
    h2_ref[...] = (x1 * _rms_scale(x1) * n2_ref[...]).astype(BF16)


def _merge(a, r, sga, sgr, xf, wpa, wph, wout, n2):
    T = xf.shape[0]
    tm = TM_PROJ
    row = lambda n: pl.BlockSpec((tm, n), lambda i: (i, 0))
    return pl.pallas_call(
        _merge_kernel,
        grid=(T // tm,),
        in_specs=[row(ATT_Q), row(HG_V), row(D_MODEL), row(D_MODEL), row(D_MODEL),
                  _const_spec((ATT_Q, D_MODEL)), _const_spec((HG_V, D_MODEL)),
                  _const_spec((D_MODEL, D_MODEL)), _const_spec((1, D_MODEL))],
        out_specs=[row(D_MODEL), row(D_MODEL)],
        out_shape=[jax.ShapeDtypeStruct((T, D_MODEL), F32),
                   jax.ShapeDtypeStruct((T, D_MODEL), BF16)],
        compiler_params=_params(),
        name="merge_out",
    )(a, r, sga, sgr, xf, wpa, wph, wout, n2)


def _ffn_kernel(h2_ref, x1_ref, wup_ref, cw_ref, cb_ref, wdn_ref, o_ref,
                carry_ref, work_ref, *, tiles_per_seq):
    tm = h2_ref.shape[0]
    halo = carry_ref.shape[0]

    @pl.when(pl.program_id(0) % tiles_per_seq == 0)
    def _():
        carry_ref[...] = jnp.zeros_like(carry_ref)

    h2 = h2_ref[...]

    def conv(slot, off):
        cols = slice(off, off + FFN_CW)
        u = _dot(h2, wup_ref[:, cols])
        work_ref[slot, 0:halo, :] = carry_ref[:, cols]
        work_ref[slot, halo:halo + tm, :] = u
        carry_ref[:, cols] = u[tm - halo:tm, :]
        w = cw_ref[:, cols]
        return (w[0:1, :] * work_ref[slot, halo - 2:halo - 2 + tm, :]
                + w[1:2, :] * work_ref[slot, halo - 1:halo - 1 + tm, :]
                + w[2:3, :] * u + cb_ref[:, cols])

    acc = x1_ref[...]
    for c in range(D_FF // FFN_CW):
        ug = conv(0, c * FFN_CW)
        uv = conv(1, D_FF + c * FFN_CW)
        act = (ug * _sigmoid(ug) * uv).astype(BF16)
        acc = acc + _dot(act, wdn_ref[c * FFN_CW:(c + 1) * FFN_CW, :])
    o_ref[...] = acc


def _ffn(h2, x1, wup, cw, cb, wdn, S):
    T = h2.shape[0]
    tm = TM_FFN
    halo = 8
    row = pl.BlockSpec((tm, D_MODEL), lambda i: (i, 0))
    return pl.pallas_call(
        functools.partial(_ffn_kernel, tiles_per_seq=S // tm),
        grid=(T // tm,),
        in_specs=[row, row, _const_spec((D_MODEL, 2 * D_FF)),
                  _const_spec((CONV_WIDTH, 2 * D_FF)), _const_spec((1, 2 * D_FF)),
                  _const_spec((D_FF, D_MODEL))],
        out_specs=row,
        out_shape=jax.ShapeDtypeStruct((T, D_MODEL), F32),
        scratch_shapes=[pltpu.VMEM((halo, 2 * D_FF), F32),
                        pltpu.VMEM((2, halo + tm, FFN_CW), F32)],
        compiler_params=_params(),
        name="conv_ffn",
    )(h2, x1, wup, cw, cb, wdn)


def _bias_table(rel_bias):
    blk = ATT_BLOCK
    assert WINDOW == blk
    d = np.arange(WINDOW)
    max_exact = N_BUCKETS // 2
    large = max_exact + (np.log(np.maximum(d, 1) / max_exact) / math.log(MAX_DISTANCE / max_exact)
                         * (N_BUCKETS - max_exact)).astype(np.int32)
    bucket = np.where(d < max_exact, d, np.minimum(large, N_BUCKETS - 1))
    onehot = jnp.asarray(bucket[:, None] == np.arange(N_BUCKETS)[None, :], F32)
    per_dist = jnp.dot(onehot, rel_bias.astype(F32), precision=lax.Precision.HIGHEST)
    heads = per_dist.shape[1]
    period = 3 * blk
    z = jnp.concatenate([jnp.full((heads, 1), -jnp.inf, F32), per_dist[::-1].T,
                         jnp.full((heads, period - WINDOW - 1), -jnp.inf, F32)], axis=1)
    skew = jnp.tile(z, (1, blk))[:, :blk * (period - 1)].reshape(heads, blk, period - 1)
    return skew[:, :, :2 * blk]


def _block_diag_ones(n, blk):
    idx = np.arange(n) // blk
    return jnp.asarray((idx[:, None] == idx[None, :]).astype(np.float32), BF16)


def kernel(x, norm1, w_in, q_norm, k_norm, sinks, rel_bias, hg_lb, hg_norm,
           w_pa, w_ph, w_out, norm2, w_up, conv_w, conv_b, w_down):
    B, S, D = x.shape
    T = B * S
    depth = w_in.shape[0]
    assert D == D_MODEL and T % TM_PROJ == 0 and S % TM_FFN == 0
    assert S % TQ_ATT == 0 and S % TB_HGRN == 0

    p_lb = jax.nn.softmax(hg_lb.astype(F32), axis=0)
    lbs = jnp.cumsum(p_lb, axis=0) - p_lb[0:1]

    bias_tab = _bias_table(rel_bias)
    bdq = _block_diag_ones(ATT_Q, ATT_HEAD_DIM)
    bdk = _block_diag_ones(ATT_KV, ATT_HEAD_DIM)
    scale = ATT_HEAD_DIM ** -0.5

    xf = x.reshape(T, D)
    for l in range(depth):
        qg = (jnp.tile(q_norm[l].astype(F32), ATT_HEADS) * scale).reshape(1, ATT_Q)
        kg = jnp.tile(k_norm[l].astype(F32), ATT_KV_HEADS).reshape(1, ATT_KV)
        (qa, ka, kb, va, vb, hq, hk, ghi, glo, hv, hgate, sga, sgr) = _inproj(
            xf, norm1[l].reshape(1, D), w_in[l].astype(BF16), qg, kg, bdq, bdk,
            lbs[l].reshape(1, HG_K))
        a = _attn(sinks[l].astype(F32), qa, ka, kb, va, vb, bias_tab, B, S)
        r = _hgrn(hq, hk, ghi, glo, hv, hgate, hg_norm[l].reshape(1, HG_DV).astype(F32), B, S)
        x1, h2 = _merge(a, r, sga, sgr, xf, w_pa[l].astype(BF16), w_ph[l].astype(BF16),
                        w_out[l].astype(BF16), norm2[l].reshape(1, D))
        xf = _ffn(h2, x1, w_up[l].astype(BF16), conv_w[l].astype(F32),
                  conv_b[l].reshape(1, 2 * D_FF).astype(F32), w_down[l].astype(BF16), S)
    return xf.reshape(B, S, D)
```

```python
import functools
import math

import numpy as np
import jax
import jax.numpy as jnp
from jax import lax
from jax.experimental import pallas as pl
from jax.experimental.pallas import tpu as pltpu

F32 = jnp.float32
BF16 = jnp.bfloat16

D_MODEL = 1024
ATT_HEADS = 8
ATT_KV_HEADS = 2
ATT_HEAD_DIM = 64
WINDOW = 128
ATT_BLOCK = 128
N_BUCKETS = 32
MAX_DISTANCE = 128
HG_HEADS = 4
HG_DK = 128
HG_DV = 128
HG_CHUNK = 64
D_FF = 2816
CONV_WIDTH = 3
EPS = 1e-6

ATT_Q = ATT_HEADS * ATT_HEAD_DIM
ATT_KV = ATT_KV_HEADS * ATT_HEAD_DIM
HG_K = HG_HEADS * HG_DK
HG_V = HG_HEADS * HG_DV
IN_SPLITS = (ATT_Q, ATT_KV, ATT_KV, HG_K, HG_K, HG_V, HG_V, D_MODEL, D_MODEL)
IN_OFFS = tuple(int(o) for o in np.cumsum((0,) + IN_SPLITS))
D_IN = IN_OFFS[-1]

LANES = 128
VMEM_LIMIT_BYTES = 56 * 1024 * 1024

TM_PROJ = 512
TM_FFN = 512
FFN_CW = 256
TQ_ATT = 512
TB_HGRN = 1024
HG_BLOCK = 256
HG_COARSE = (128, 64)
HG_LEVELS = (32, 16, 8, 4, 2, 1)
HG_SAFE_SPAN = 60.0


def _dot(a, b):
    return jnp.dot(a, b, preferred_element_type=F32)


def _dot_nt(a, b):
    return lax.dot_general(a, b, (((1,), (1,)), ((), ())), preferred_element_type=F32)


def _dot_tn(a, b):
    return lax.dot_general(a, b, (((0,), (0,)), ((), ())), preferred_element_type=F32)


def _sigmoid(x):
    return 1.0 / (1.0 + jnp.exp(-x))


def _rms_scale(x):
    return lax.rsqrt(jnp.mean(x * x, axis=-1, keepdims=True) + EPS)


def _const_spec(shape):
    nd = len(shape)
    return pl.BlockSpec(shape, lambda *_: (0,) * nd)


def _params():
    return pltpu.CompilerParams(vmem_limit_bytes=VMEM_LIMIT_BYTES)


def _inproj_kernel(x_ref, n1_ref, w_ref, qg_ref, kg_ref, bdq_ref, bdk_ref, lb_ref,
                   qa_ref, ka_ref, kb_ref, va_ref, vb_ref,
                   hq_ref, hk_ref, ghi_ref, glo_ref, hv_ref, hgate_ref, sga_ref, sgr_ref):
    x = x_ref[...]
    h = (x * _rms_scale(x) * n1_ref[...]).astype(BF16)

    def proj(seg):
        return _dot(h, w_ref[:, IN_OFFS[seg]:IN_OFFS[seg + 1]])

    q = proj(0)
    q_ms = _dot((q * q).astype(BF16), bdq_ref[...]) * (1.0 / ATT_HEAD_DIM)
    qa_ref[...] = (q * lax.rsqrt(q_ms + EPS) * qg_ref[...]).astype(BF16)

    k = proj(1)
    k_ms = _dot((k * k).astype(BF16), bdk_ref[...]) * (1.0 / ATT_HEAD_DIM)
    kn = k * lax.rsqrt(k_ms + EPS) * kg_ref[...]
    ka_ref[...] = kn.astype(BF16)
    kb_ref[...] = pltpu.roll(kn, ATT_HEAD_DIM, 1).astype(BF16)

    v = proj(2)
    va_ref[...] = v.astype(BF16)
    vb_ref[...] = pltpu.roll(v, ATT_HEAD_DIM, 1).astype(BF16)

    qr = proj(3)
    hq_ref[...] = (qr * _sigmoid(qr)).astype(BF16)
    lb = lb_ref[...]
    f = lb + (1.0 - lb) * _sigmoid(proj(4))
    hk_ref[...] = (1.0 - f).astype(BF16)
    g = jnp.log(f)
    g_hi = g.astype(BF16)
    ghi_ref[...] = g_hi
    glo_ref[...] = (g - g_hi.astype(F32)).astype(BF16)
    hv_ref[...] = proj(5).astype(BF16)
    gr = proj(6)
    hgate_ref[...] = (gr * _sigmoid(gr)).astype(BF16)
    sga_ref[...] = _sigmoid(proj(7)).astype(BF16)
    sgr_ref[...] = _sigmoid(proj(8)).astype(BF16)


def _inproj(xf, n1, w_in, qg, kg, bdq, bdk, lb):
    T = xf.shape[0]
    tm = TM_PROJ
    row = lambda n: pl.BlockSpec((tm, n), lambda i: (i, 0))
    out_widths = [ATT_Q, ATT_KV, ATT_KV, ATT_KV, ATT_KV,
                  HG_K, HG_K, HG_K, HG_K, HG_V, HG_V, D_MODEL, D_MODEL]
    return pl.pallas_call(
        _inproj_kernel,
        grid=(T // tm,),
        in_specs=[row(D_MODEL), _const_spec((1, D_MODEL)), _const_spec((D_MODEL, D_IN)),
                  _const_spec((1, ATT_Q)), _const_spec((1, ATT_KV)),
                  _const_spec((ATT_Q, ATT_Q)), _const_spec((ATT_KV, ATT_KV)),
                  _const_spec((1, HG_K))],
        out_specs=[row(n) for n in out_widths],
        out_shape=[jax.ShapeDtypeStruct((T, n), BF16) for n in out_widths],
        compiler_params=_params(),
        name="inproj",
    )(xf, n1, w_in, qg, kg, bdq, bdk, lb)


def _attn_kernel(sink_ref, q_ref, kac_ref, kbc_ref, vac_ref, vbc_ref,
                 kap_ref, kbp_ref, vap_ref, vbp_ref, bias_ref, o_ref,
                 ka_s, kb_s, va_s, vb_s):
    blk = ATT_BLOCK
    n_qb = q_ref.shape[0] // blk
    seq_blk = pl.program_id(1)

    for s_ref, p_ref, c_ref in ((ka_s, kap_ref, kac_ref), (kb_s, kbp_ref, kbc_ref),
                                (va_s, vap_ref, vac_ref), (vb_s, vbp_ref, vbc_ref)):
        s_ref[0:blk, :] = p_ref[...]
        s_ref[blk:, :] = c_ref[...]

    lane_q = lax.broadcasted_iota(jnp.int32, (blk, LANES), 1)
    lane_kv = lax.broadcasted_iota(jnp.int32, (2 * blk, LANES), 1)
    col = lax.broadcasted_iota(jnp.int32, (blk, 2 * blk), 1)
    lo_q = lane_q < ATT_HEAD_DIM
    q_lo = jnp.where(lo_q, 1.0, 0.0).astype(BF16)
    q_hi = jnp.where(lo_q, 0.0, 1.0).astype(BF16)
    ind_lo = jnp.where(lane_kv < ATT_HEAD_DIM, 1.0, 0.0).astype(BF16)
    ind_hi = jnp.where(lane_kv < ATT_HEAD_DIM, 0.0, 1.0).astype(BF16)
    no_prev = jnp.where(seq_blk == 0, -jnp.inf, 0.0).astype(F32)

    for qb in range(n_qb):
        r0 = qb * blk
        band = {"a": (ka_s[r0:r0 + 2 * blk, :], va_s[r0:r0 + 2 * blk, :]),
                "b": (kb_s[r0:r0 + 2 * blk, :], vb_s[r0:r0 + 2 * blk, :])}
        for g in range(ATT_HEADS // 2):
            even_src, odd_src = ("a", "b") if g < 2 else ("b", "a")
            q2 = q_ref[r0:r0 + blk, g * LANES:(g + 1) * LANES]
            probs, sink_terms = [], []
            for half, src in ((0, even_src), (1, odd_src)):
                head = 2 * g + half
                qh = q2 * (q_lo if half == 0 else q_hi)
                s = _dot_nt(qh, band[src][0]) + bias_ref[head]
                if qb == 0:
                    s = s + jnp.where(col < blk, no_prev, 0.0)
                sink = sink_ref[head]
                m = jnp.maximum(jnp.max(s, axis=-1, keepdims=True), sink)
                probs.append(jnp.exp(s - m).astype(BF16))
                sink_terms.append(jnp.exp(sink - m))
            p_cat = jnp.concatenate(probs, axis=1)
            v_even = band[even_src][1] * ind_lo
            v_odd = band[odd_src][1] * ind_hi
            v_stack = jnp.concatenate(
                [jnp.concatenate([v_even, ind_lo], axis=1),
                 jnp.concatenate([v_odd, ind_hi], axis=1)], axis=0)
            res = _dot(p_cat, v_stack)
            den = res[:, LANES:] + jnp.where(lo_q, sink_terms[0], sink_terms[1])
            o_ref[r0:r0 + blk, g * LANES:(g + 1) * LANES] = (res[:, :LANES] / den).astype(BF16)


def _attn(sinks, qa, ka, kb, va, vb, bias_tab, B, S):
    T = qa.shape[0]
    tq = TQ_ATT
    blk = ATT_BLOCK
    per_seq = S // tq
    ratio = tq // blk
    cur = lambda n: pl.BlockSpec((tq, n), lambda b, i: (b * per_seq + i, 0))
    prev = pl.BlockSpec(
        (blk, ATT_KV), lambda b, i: (b * (S // blk) + jnp.maximum(i * ratio - 1, 0), 0))
    return pl.pallas_call(
        _attn_kernel,
        grid=(B, per_seq),
        in_specs=[pl.BlockSpec(memory_space=pltpu.SMEM), cur(ATT_Q),
                  cur(ATT_KV), cur(ATT_KV), cur(ATT_KV), cur(ATT_KV),
                  prev, prev, prev, prev,
                  _const_spec((ATT_HEADS, blk, 2 * blk))],
        out_specs=cur(ATT_Q),
        out_shape=jax.ShapeDtypeStruct((T, ATT_Q), BF16),
        scratch_shapes=[pltpu.VMEM((tq + blk, ATT_KV), BF16)] * 4,
        compiler_params=_params(),
        name="swa_attn",
    )(sinks, qa, ka, kb, va, vb, ka, kb, va, vb, bias_tab)


def _hgrn_masks():
    n = HG_BLOCK
    t = np.arange(n)[:, None]
    s = np.arange(n)[None, :]

    def level(m):
        return ((t // (2 * m)) == (s // (2 * m))) & ((t // m) % 2 == 1) & ((s // m) % 2 == 0)

    masks = [level(m) for m in HG_COARSE]
    masks.append(((t // HG_CHUNK) == (s // HG_CHUNK)) & (s <= t))
    masks += [level(m) for m in HG_LEVELS] + [t == s]
    return np.stack(masks, axis=0).astype(np.float32), (s <= t).astype(np.float32)


def _hgrn_kernel(q_ref, k_ref, ghi_ref, glo_ref, v_ref, gate_ref, hn_ref, tri_ref, mask_ref,
                 o_ref, st_ref):
    N = HG_BLOCK
    W = q_ref.shape[1]
    n_blocks = q_ref.shape[0] // N
    heads = [slice(h * HG_DK, (h + 1) * HG_DK) for h in range(HG_HEADS)]
    n_coarse = len(HG_COARSE)

    @pl.when(pl.program_id(1) == 0)
    def _():
        st_ref[...] = jnp.zeros_like(st_ref)

    hn = hn_ref[...]
    tri = tri_ref[...]

    def block(c, carry):
        r0 = pl.multiple_of(c * N, N)
        rows = pl.ds(r0, N)
        qq = q_ref[rows, :]
        kk = k_ref[rows, :]
        vv = v_ref[rows, :]
        g_hi = ghi_ref[rows, :]
        g_lo = glo_ref[rows, :]

        b = _dot(tri, g_hi) + _dot(tri, g_lo)
        b_last = b[N - 1:N, :]
        decay = jnp.exp(b_last)

        def scaled(e):
            eb = e.astype(BF16)
            return qq * eb, kk * eb

        def offset(m):
            ref = [jnp.broadcast_to(b[k + m - 1:k + m, :], (2 * m, W)) for k in range(0, N, 2 * m)]
            return b - (ref[0] if len(ref) == 1 else jnp.concatenate(ref, axis=0))

        def finish(h, att_fine, att_coarse, o_state):
            sl = heads[h]
            o = _dot((att_fine + att_coarse).astype(BF16), vv[:, sl]) + o_state
            r = o * _rms_scale(o) * hn
            o_ref[rows, sl] = (r * gate_ref[rows, sl].astype(F32)).astype(BF16)

        q_state = qq * jnp.exp(b).astype(BF16)
        k_state = kk * jnp.exp(b_last - b).astype(BF16)
        coarse = [scaled(jnp.exp(-jnp.abs(offset(m)))) for m in HG_COARSE]

        d_mid = offset(HG_CHUNK // 2)
        span = jnp.max(jnp.abs(d_mid))
        e_mid = jnp.exp(d_mid)
        q_mid = qq * e_mid.astype(BF16)
        k_mid = kk * (1.0 / e_mid).astype(BF16)
        in_chunk = mask_ref[n_coarse] > 0.5

        att_coarse, o_state = [], []
        for h, sl in enumerate(heads):
            acc = _dot_nt(coarse[0][0][:, sl], coarse[0][1][:, sl]) * mask_ref[0]
            for i in range(1, n_coarse):
                acc = acc + _dot_nt(coarse[i][0][:, sl], coarse[i][1][:, sl]) * mask_ref[i]
            att_coarse.append(acc)
            st = st_ref[h]
            o_state.append(_dot_nt(q_state[:, sl], st.astype(BF16)))
            st_ref[h] = st * decay[:, sl] + _dot_tn(vv[:, sl], k_state[:, sl])
            fine = jnp.where(in_chunk, _dot_nt(q_mid[:, sl], k_mid[:, sl]), 0.0)
            finish(h, fine, acc, o_state[h])

        @pl.when(span >= HG_SAFE_SPAN)
        def _():
            f = jnp.exp(g_hi.astype(F32) + g_lo.astype(F32))
            scales = [jnp.exp(-jnp.abs(d_mid))]
            scales += [jnp.exp(-jnp.abs(offset(m))) for m in HG_LEVELS[1:] if 2 * m >= 8]
            row = lax.broadcasted_iota(jnp.int32, (N, W), 0)
            f_next = pltpu.roll(f, N - 1, 0)
            f_prev = pltpu.roll(f, 1, 0)
            r4 = row & 3
            scales.append(jnp.where(r4 == 0, f_next,
                                    jnp.where(r4 == 1, 1.0, jnp.where(r4 == 2, f, f * f_prev))))
            scales.append(jnp.where((row & 1) == 1, f, 1.0))
            pairs = [scaled(e) for e in scales] + [(qq, kk)]
            for h, sl in enumerate(heads):
                fine = jnp.zeros((N, N), F32)
                for i, (ql, kl) in enumerate(pairs):
                    fine = fine + _dot_nt(ql[:, sl], kl[:, sl]) * mask_ref[n_coarse + 1 + i]
                finish(h, fine, att_coarse[h], o_state[h])

        return carry

    lax.fori_loop(0, n_blocks, block, 0)


def _hgrn(hq, hk, ghi, glo, hv, hgate, hn, B, S):
    T = hq.shape[0]
    masks_np, tri_np = _hgrn_masks()
    masks = jnp.asarray(masks_np, F32)
    tri = jnp.asarray(tri_np, BF16)
    tb = TB_HGRN
    per_seq = S // tb
    row = pl.BlockSpec((tb, HG_K), lambda b, i: (b * per_seq + i, 0))
    return pl.pallas_call(
        _hgrn_kernel,
        grid=(B, per_seq),
        in_specs=[row, row, row, row, row, row, _const_spec((1, HG_DV)),
                  _const_spec(tri.shape), _const_spec(masks.shape)],
        out_specs=row,
        out_shape=jax.ShapeDtypeStruct((T, HG_V), BF16),
        scratch_shapes=[pltpu.VMEM((HG_HEADS, HG_DV, HG_DK), F32)],
        compiler_params=_params(),
        name="hgrn2",
    )(hq, hk, ghi, glo, hv, hgate, hn, tri, masks)


def _merge_kernel(a_ref, r_ref, sga_ref, sgr_ref, x_ref, wpa_ref, wph_ref, wout_ref, n2_ref,
                  x1_ref, h2_ref):
    pa = _dot(a_ref[...], wpa_ref[...])
    ph = _dot(r_ref[...], wph_ref[...])
    merged = sga_ref[...].astype(F32) * pa + sgr_ref[...].astype(F32) * ph
    x1 = x_ref[...] + _dot(merged.astype(BF16), wout_ref[...])
    x1_ref[...] = x1
    h2_ref[...] = (x1 * _rms_scale(x1) * n2_ref[...]).astype(BF16)


def _merge(a, r, sga, sgr, xf, wpa, wph, wout, n2):
    T = xf.shape[0]
    tm = TM_PROJ
    row = lambda n: pl.BlockSpec((tm, n), lambda i: (i, 0))
    return pl.pallas_call(
        _merge_kernel,
        grid=(T // tm,),
        in_specs=[row(ATT_Q), row(HG_V), row(D_MODEL), row(D_MODEL), row(D_MODEL),
                  _const_spec((ATT_Q, D_MODEL)), _const_spec((HG_V, D_MODEL)),
                  _const_spec((D_MODEL, D_MODEL)), _const_spec((1, D_MODEL))],
        out_specs=[row(D_MODEL), row(D_MODEL)],
        out_shape=[jax.ShapeDtypeStruct((T, D_MODEL), F32),
                   jax.ShapeDtypeStruct((T, D_MODEL), BF16)],
        compiler_params=_params(),
        name="merge_out",
    )(a, r, sga, sgr, xf, wpa, wph, wout, n2)


def _ffn_kernel(h2_ref, x1_ref, wup_ref, cw_ref, cb_ref, wdn_ref, o_ref,
                carry_ref, act_ref, *, tiles_per_seq):
    tm = h2_ref.shape[0]
    halo = carry_ref.shape[0]

    @pl.when(pl.program_id(0) % tiles_per_seq == 0)
    def _():
        carry_ref[...] = jnp.zeros_like(carry_ref)

    h2 = h2_ref[...]
    head_row = lax.broadcasted_iota(jnp.int32, (halo, FFN_CW), 0)

    def shifted(u, prev, n):
        s = pltpu.roll(u, n, 0)
        head = jnp.where(head_row < n, pltpu.roll(prev, n, 0), s[0:halo, :])
        return jnp.concatenate([head, s[halo:, :]], axis=0)

    def conv(off):
        cols = slice(off, off + FFN_CW)
        u = _dot(h2, wup_ref[:, cols])
        prev = carry_ref[:, cols]
        carry_ref[:, cols] = u[tm - halo:tm, :]
        w = cw_ref[:, cols]
        return (w[0:1, :] * shifted(u, prev, 2) + w[1:2, :] * shifted(u, prev, 1)
                + w[2:3, :] * u + cb_ref[:, cols])

    for c in range(D_FF // FFN_CW):
        ug = conv(c * FFN_CW)
        uv = conv(D_FF + c * FFN_CW)
        act_ref[:, c * FFN_CW:(c + 1) * FFN_CW] = (ug * _sigmoid(ug) * uv).astype(BF16)
    o_ref[...] = x1_ref[...] + _dot(act_ref[...], wdn_ref[...])


def _ffn(h2, x1, wup, cw, cb, wdn, S):
    T = h2.shape[0]
    tm = TM_FFN
    halo = 8
    row = pl.BlockSpec((tm, D_MODEL), lambda i: (i, 0))
    return pl.pallas_call(
        functools.partial(_ffn_kernel, tiles_per_seq=S // tm),
        grid=(T // tm,),
        in_specs=[row, row, _const_spec((D_MODEL, 2 * D_FF)),
                  _const_spec((CONV_WIDTH, 2 * D_FF)), _const_spec((1, 2 * D_FF)),
                  _const_spec((D_FF, D_MODEL))],
        out_specs=row,
        out_shape=jax.ShapeDtypeStruct((T, D_MODEL), F32),
        scratch_shapes=[pltpu.VMEM((halo, 2 * D_FF), F32),
                        pltpu.VMEM((tm, D_FF), BF16)],
        compiler_params=_params(),
        name="conv_ffn",
    )(h2, x1, wup, cw, cb, wdn)


def _bias_table(rel_bias):
    blk = ATT_BLOCK
    assert WINDOW == blk
    d = np.arange(WINDOW)
    max_exact = N_BUCKETS // 2
    large = max_exact + (np.log(np.maximum(d, 1) / max_exact) / math.log(MAX_DISTANCE / max_exact)
                         * (N_BUCKETS - max_exact)).astype(np.int32)
    bucket = np.where(d < max_exact, d, np.minimum(large, N_BUCKETS - 1))
    onehot = jnp.asarray(bucket[:, None] == np.arange(N_BUCKETS)[None, :], F32)
    per_dist = jnp.dot(onehot, rel_bias.astype(F32), precision=lax.Precision.HIGHEST)
    heads = per_dist.shape[1]
    period = 3 * blk
    z = jnp.concatenate([jnp.full((heads, 1), -jnp.inf, F32), per_dist[::-1].T,
                         jnp.full((heads, period - WINDOW - 1), -jnp.inf, F32)], axis=1)
    skew = jnp.tile(z, (1, blk))[:, :blk * (period - 1)].reshape(heads, blk, period - 1)
    return skew[:, :, :2 * blk]


def _block_diag_ones(n, blk):
    idx = np.arange(n) // blk
    return jnp.asarray((idx[:, None] == idx[None, :]).astype(np.float32), BF16)


def kernel(x, norm1, w_in, q_norm, k_norm, sinks, rel_bias, hg_lb, hg_norm,
           w_pa, w_ph, w_out, norm2, w_up, conv_w, conv_b, w_down):
    B, S, D = x.shape
    T = B * S
    depth = w_in.shape[0]
    assert D == D_MODEL and T % TM_PROJ == 0 and S % TM_FFN == 0
    assert S % TQ_ATT == 0 and S % TB_HGRN == 0

    p_lb = jax.nn.softmax(hg_lb.astype(F32), axis=0)
    lbs = jnp.cumsum(p_lb, axis=0) - p_lb[0:1]

    bias_tab = _bias_table(rel_bias)
    bdq = _block_diag_ones(ATT_Q, ATT_HEAD_DIM)
    bdk = _block_diag_ones(ATT_KV, ATT_HEAD_DIM)
    scale = ATT_HEAD_DIM ** -0.5

    xf = x.reshape(T, D)
    for l in range(depth):
        qg = (jnp.tile(q_norm[l].astype(F32), ATT_HEADS) * scale).reshape(1, ATT_Q)
        kg = jnp.tile(k_norm[l].astype(F32), ATT_KV_HEADS).reshape(1, ATT_KV)
        (qa, ka, kb, va, vb, hq, hk, ghi, glo, hv, hgate, sga, sgr) = _inproj(
            xf, norm1[l].reshape(1, D), w_in[l].astype(BF16), qg, kg, bdq, bdk,
            lbs[l].reshape(1, HG_K))
        a = _attn(sinks[l].astype(F32), qa, ka, kb, va, vb, bias_tab, B, S)
        r = _hgrn(hq, hk, ghi, glo, hv, hgate, hg_norm[l].reshape(1, HG_DV).astype(F32), B, S)
        x1, h2 = _merge(a, r, sga, sgr, xf, w_pa[l].astype(BF16), w_ph[l].astype(BF16),
                        w_out[l].astype(BF16), norm2[l].reshape(1, D))
        xf = _ffn(h2, x1, w_up[l].astype(BF16), conv_w[l].astype(F32),
                  conv_b[l].reshape(1, 2 * D_FF).astype(F32), w_down[l].astype(BF16), S)
    return xf.reshape(B, S, D)
```

```python
import functools
import math

import numpy as np
import jax
import jax.numpy as jnp
from jax import lax
from jax.experimental import pallas as pl
from jax.experimental.pallas import tpu as pltpu

F32 = jnp.float32
BF16 = jnp.bfloat16

D_MODEL = 1024
ATT_HEADS = 8
ATT_KV_HEADS = 2
ATT_HEAD_DIM = 64
WINDOW = 128
ATT_BLOCK = 128
N_BUCKETS = 32
MAX_DISTANCE = 128
HG_HEADS = 4
HG_DK = 128
HG_DV = 128
HG_CHUNK = 64
D_FF = 2816
CONV_WIDTH = 3
EPS = 1e-6

ATT_Q = ATT_HEADS * ATT_HEAD_DIM
ATT_KV = ATT_KV_HEADS * ATT_HEAD_DIM
HG_K = HG_HEADS * HG_DK
HG_V = HG_HEADS * HG_DV
IN_SPLITS = (ATT_Q, ATT_KV, ATT_KV, HG_K, HG_K, HG_V, HG_V, D_MODEL, D_MODEL)
IN_OFFS = tuple(int(o) for o in np.cumsum((0,) + IN_SPLITS))
D_IN = IN_OFFS[-1]

LANES = 128
VMEM_LIMIT_BYTES = 56 * 1024 * 1024

TM_PROJ = 512
TM_FFN = 512
FFN_CW = 256
TM_MIX = 512
HG_BLOCK = 256
HG_COARSE = (128, 64)
HG_LEVELS = (32, 16, 8, 4, 2, 1)
HG_SAFE_SPAN = 60.0


def _dot(a, b):
    return jnp.dot(a, b, preferred_element_type=F32)


def _dot_nt(a, b):
    return lax.dot_general(a, b, (((1,), (1,)), ((), ())), preferred_element_type=F32)


def _dot_tn(a, b):
    return lax.dot_general(a, b, (((0,), (0,)), ((), ())), preferred_element_type=F32)


def _sigmoid(x):
    return 1.0 / (1.0 + jnp.exp(-x))


def _rms_scale(x):
    return lax.rsqrt(jnp.mean(x * x, axis=-1, keepdims=True) + EPS)


def _const_spec(shape):
    nd = len(shape)
    return pl.BlockSpec(shape, lambda *_: (0,) * nd)


def _params():
    return pltpu.CompilerParams(vmem_limit_bytes=VMEM_LIMIT_BYTES)


def _inproj_kernel(x_ref, n1_ref, w_ref, qg_ref, kg_ref, bdq_ref, bdk_ref, lb_ref,
                   qa_ref, ka_ref, kb_ref, va_ref, vb_ref,
                   hq_ref, hk_ref, ghi_ref, glo_ref, hv_ref, hgate_ref, sga_ref, sgr_ref):
    x = x_ref[...]
    h = (x * _rms_scale(x) * n1_ref[...]).astype(BF16)

    def proj(seg):
        return _dot(h, w_ref[:, IN_OFFS[seg]:IN_OFFS[seg + 1]])

    q = proj(0)
    q_ms = _dot((q * q).astype(BF16), bdq_ref[...]) * (1.0 / ATT_HEAD_DIM)
    qa_ref[...] = (q * lax.rsqrt(q_ms + EPS) * qg_ref[...]).astype(BF16)

    k = proj(1)
    k_ms = _dot((k * k).astype(BF16), bdk_ref[...]) * (1.0 / ATT_HEAD_DIM)
    kn = k * lax.rsqrt(k_ms + EPS) * kg_ref[...]
    ka_ref[...] = kn.astype(BF16)
    kb_ref[...] = pltpu.roll(kn, ATT_HEAD_DIM, 1).astype(BF16)

    v = proj(2)
    va_ref[...] = v.astype(BF16)
    vb_ref[...] = pltpu.roll(v, ATT_HEAD_DIM, 1).astype(BF16)

    qr = proj(3)
    hq_ref[...] = (qr * _sigmoid(qr)).astype(BF16)
    lb = lb_ref[...]
    f = lb + (1.0 - lb) * _sigmoid(proj(4))
    hk_ref[...] = (1.0 - f).astype(BF16)
    g = jnp.log(f)
    g_hi = g.astype(BF16)
    ghi_ref[...] = g_hi
    glo_ref[...] = (g - g_hi.astype(F32)).astype(BF16)
    hv_ref[...] = proj(5).astype(BF16)
    gr = proj(6)
    hgate_ref[...] = (gr * _sigmoid(gr)).astype(BF16)
    sga_ref[...] = _sigmoid(proj(7)).astype(BF16)
    sgr_ref[...] = _sigmoid(proj(8)).astype(BF16)


def _inproj(xf, n1, w_in, qg, kg, bdq, bdk, lb):
    T = xf.shape[0]
    tm = TM_PROJ
    row = lambda n: pl.BlockSpec((tm, n), lambda i: (i, 0))
    out_widths = [ATT_Q, ATT_KV, ATT_KV, ATT_KV, ATT_KV,
                  HG_K, HG_K, HG_K, HG_K, HG_V, HG_V, D_MODEL, D_MODEL]
    return pl.pallas_call(
        _inproj_kernel,
        grid=(T // tm,),
        in_specs=[row(D_MODEL), _const_spec((1, D_MODEL)), _const_spec((D_MODEL, D_IN)),
                  _const_spec((1, ATT_Q)), _const_spec((1, ATT_KV)),
                  _const_spec((ATT_Q, ATT_Q)), _const_spec((ATT_KV, ATT_KV)),
                  _const_spec((1, HG_K))],
        out_specs=[row(n) for n in out_widths],
        out_shape=[jax.ShapeDtypeStruct((T, n), BF16) for n in out_widths],
        compiler_params=_params(),
        name="inproj",
    )(xf, n1, w_in, qg, kg, bdq, bdk, lb)


def _hgrn_masks():
    n = HG_BLOCK
    t = np.arange(n)[:, None]
    s = np.arange(n)[None, :]

    def level(m):
        return ((t // (2 * m)) == (s // (2 * m))) & ((t // m) % 2 == 1) & ((s // m) % 2 == 0)

    masks = [level(m) for m in HG_COARSE]
    masks.append(((t // HG_CHUNK) == (s // HG_CHUNK)) & (s <= t))
    masks += [level(m) for m in HG_LEVELS] + [t == s]
    return np.stack(masks, axis=0).astype(np.float32), (s <= t).astype(np.float32)


def _attention_rows(r0, n_rows, first_in_seq, sink_ref, q_ref, bias_ref, bands, a_ref):
    blk = ATT_BLOCK
    ka_s, kb_s, va_s, vb_s = bands
    lane_q = lax.broadcasted_iota(jnp.int32, (blk, LANES), 1)
    lane_kv = lax.broadcasted_iota(jnp.int32, (2 * blk, LANES), 1)
    col = lax.broadcasted_iota(jnp.int32, (blk, 2 * blk), 1)
    lo_q = lane_q < ATT_HEAD_DIM
    q_lo = jnp.where(lo_q, 1.0, 0.0).astype(BF16)
    q_hi = jnp.where(lo_q, 0.0, 1.0).astype(BF16)
    ind_lo = jnp.where(lane_kv < ATT_HEAD_DIM, 1.0, 0.0).astype(BF16)
    ind_hi = jnp.where(lane_kv < ATT_HEAD_DIM, 0.0, 1.0).astype(BF16)
    no_prev = jnp.where(first_in_seq, -jnp.inf, 0.0).astype(F32)

    for q0 in range(r0, r0 + n_rows, blk):
        band = {"a": (ka_s[q0:q0 + 2 * blk, :], va_s[q0:q0 + 2 * blk, :]),
                "b": (kb_s[q0:q0 + 2 * blk, :], vb_s[q0:q0 + 2 * blk, :])}
        for g in range(ATT_HEADS // 2):
            even_src, odd_src = ("a", "b") if g < 2 else ("b", "a")
            q2 = q_ref[q0:q0 + blk, g * LANES:(g + 1) * LANES]
            probs, sink_terms = [], []
            for half, src in ((0, even_src), (1, odd_src)):
                head = 2 * g + half
                qh = q2 * (q_lo if half == 0 else q_hi)
                s = _dot_nt(qh, band[src][0]) + bias_ref[head]
                if q0 == 0:
                    s = s + jnp.where(col < blk, no_prev, 0.0)
                sink = sink_ref[head]
                m = jnp.maximum(jnp.max(s, axis=-1, keepdims=True), sink)
                probs.append(jnp.exp(s - m).astype(BF16))
                sink_terms.append(jnp.exp(sink - m))
            p_cat = jnp.concatenate(probs, axis=1)
            v_even = band[even_src][1] * ind_lo
            v_odd = band[odd_src][1] * ind_hi
            v_stack = jnp.concatenate(
                [jnp.concatenate([v_even, ind_lo], axis=1),
                 jnp.concatenate([v_odd, ind_hi], axis=1)], axis=0)
            res = _dot(p_cat, v_stack)
            den = res[:, LANES:] + jnp.where(lo_q, sink_terms[0], sink_terms[1])
            a_ref[q0:q0 + blk, g * LANES:(g + 1) * LANES] = (res[:, :LANES] / den).astype(BF16)


def _hgrn_block(r0, q_ref, k_ref, ghi_ref, glo_ref, v_ref, gate_ref, hn, tri, mask_ref,
                st_ref, r_ref):
    N = HG_BLOCK
    W = q_ref.shape[1]
    rows = slice(r0, r0 + N)
    heads = [slice(h * HG_DK, (h + 1) * HG_DK) for h in range(HG_HEADS)]
    n_coarse = len(HG_COARSE)
    qq = q_ref[rows, :]
    kk = k_ref[rows, :]
    vv = v_ref[rows, :]
    g_hi = ghi_ref[rows, :]
    g_lo = glo_ref[rows, :]

    b = _dot(tri, g_hi) + _dot(tri, g_lo)
    b_last = b[N - 1:N, :]
    decay = jnp.exp(b_last)

    def scaled(e):
        eb = e.astype(BF16)
        return qq * eb, kk * eb

    def offset(m):
        ref = [jnp.broadcast_to(b[k + m - 1:k + m, :], (2 * m, W)) for k in range(0, N, 2 * m)]
        return b - (ref[0] if len(ref) == 1 else jnp.concatenate(ref, axis=0))

    def finish(h, att_fine, att_coarse, o_state):
        sl = heads[h]
        o = _dot((att_fine + att_coarse).astype(BF16), vv[:, sl]) + o_state
        r = o * _rms_scale(o) * hn
        r_ref[rows, sl] = (r * gate_ref[rows, sl].astype(F32)).astype(BF16)

    q_state = qq * jnp.exp(b).astype(BF16)
    k_state = kk * jnp.exp(b_last - b).astype(BF16)
    coarse = [scaled(jnp.exp(-jnp.abs(offset(m)))) for m in HG_COARSE]

    d_mid = offset(HG_CHUNK // 2)
    span = jnp.max(jnp.abs(d_mid))
    e_mid = jnp.exp(d_mid)
    q_mid = qq * e_mid.astype(BF16)
    k_mid = kk * (1.0 / e_mid).astype(BF16)
    in_chunk = mask_ref[n_coarse] > 0.5

    att_coarse, o_state = [], []
    for h, sl in enumerate(heads):
        acc = _dot_nt(coarse[0][0][:, sl], coarse[0][1][:, sl]) * mask_ref[0]
        for i in range(1, n_coarse):
            acc = acc + _dot_nt(coarse[i][0][:, sl], coarse[i][1][:, sl]) * mask_ref[i]
        att_coarse.append(acc)
        st = st_ref[h]
        o_state.append(_dot_nt(q_state[:, sl], st.astype(BF16)))
        st_ref[h] = st * decay[:, sl] + _dot_tn(vv[:, sl], k_state[:, sl])
        fine = jnp.where(in_chunk, _dot_nt(q_mid[:, sl], k_mid[:, sl]), 0.0)
        finish(h, fine, acc, o_state[h])

    @pl.when(span >= HG_SAFE_SPAN)
    def _():
        f = jnp.exp(g_hi.astype(F32) + g_lo.astype(F32))
        scales = [jnp.exp(-jnp.abs(d_mid))]
        scales += [jnp.exp(-jnp.abs(offset(m))) for m in HG_LEVELS[1:] if 2 * m >= 8]
        row = lax.broadcasted_iota(jnp.int32, (N, W), 0)
        f_next = pltpu.roll(f, N - 1, 0)
        f_prev = pltpu.roll(f, 1, 0)
        r4 = row & 3
        scales.append(jnp.where(r4 == 0, f_next,
                                jnp.where(r4 == 1, 1.0, jnp.where(r4 == 2, f, f * f_prev))))
        scales.append(jnp.where((row & 1) == 1, f, 1.0))
        pairs = [scaled(e) for e in scales] + [(qq, kk)]
        for h, sl in enumerate(heads):
            fine = jnp.zeros((N, N), F32)
            for i, (ql, kl) in enumerate(pairs):
                fine = fine + _dot_nt(ql[:, sl], kl[:, sl]) * mask_ref[n_coarse + 1 + i]
            finish(h, fine, att_coarse[h], o_state[h])


def _merge_rows(rows, a_ref, r_ref, sga_ref, sgr_ref, x_ref, wpa_ref, wph_ref, wout_ref, n2,
                x1_ref, h2_ref):
    pa = _dot(a_ref[rows, :], wpa_ref[...])
    ph = _dot(r_ref[rows, :], wph_ref[...])
    merged = sga_ref[rows, :].astype(F32) * pa + sgr_ref[rows, :].astype(F32) * ph
    x1 = x_ref[rows, :] + _dot(merged.astype(BF16), wout_ref[...])
    x1_ref[rows, :] = x1
    h2_ref[rows, :] = (x1 * _rms_scale(x1) * n2).astype(BF16)


def _mixer_kernel(sink_ref, qa_ref, kac_ref, kbc_ref, vac_ref, vbc_ref,
                  kap_ref, kbp_ref, vap_ref, vbp_ref, bias_ref,
                  hq_ref, hk_ref, ghi_ref, glo_ref, hv_ref, hgate_ref, hn_ref, tri_ref, mask_ref,
                  sga_ref, sgr_ref, x_ref, wpa_ref, wph_ref, wout_ref, n2_ref,
                  x1_ref, h2_ref,
                  ka_s, kb_s, va_s, vb_s, a_s, r_s, st_ref):
    blk = ATT_BLOCK
    tile = qa_ref.shape[0]
    first_in_seq = pl.program_id(1) == 0

    @pl.when(first_in_seq)
    def _():
        st_ref[...] = jnp.zeros_like(st_ref)

    for s_ref, p_ref, c_ref in ((ka_s, kap_ref, kac_ref), (kb_s, kbp_ref, kbc_ref),
                                (va_s, vap_ref, vac_ref), (vb_s, vbp_ref, vbc_ref)):
        s_ref[0:blk, :] = p_ref[...]
        s_ref[blk:, :] = c_ref[...]

    hn = hn_ref[...]
    tri = tri_ref[...]
    n2 = n2_ref[...]
    for r0 in range(0, tile, HG_BLOCK):
        _attention_rows(r0, HG_BLOCK, first_in_seq, sink_ref, qa_ref, bias_ref,
                        (ka_s, kb_s, va_s, vb_s), a_s)
        _hgrn_block(r0, hq_ref, hk_ref, ghi_ref, glo_ref, hv_ref, hgate_ref, hn, tri, mask_ref,
                    st_ref, r_s)
        _merge_rows(slice(r0, r0 + HG_BLOCK), a_s, r_s, sga_ref, sgr_ref, x_ref,
                    wpa_ref, wph_ref, wout_ref, n2, x1_ref, h2_ref)


def _mixer(sinks, qa, ka, kb, va, vb, bias_tab, hq, hk, ghi, glo, hv, hgate, hn,
           sga, sgr, xf, wpa, wph, wout, n2, B, S):
    T = qa.shape[0]
    tm = TM_MIX
    blk = ATT_BLOCK
    per_seq = S // tm
    ratio = tm // blk
    masks_np, tri_np = _hgrn_masks()
    masks = jnp.asarray(masks_np, F32)
    tri = jnp.asarray(tri_np, BF16)
    cur = lambda n: pl.BlockSpec((tm, n), lambda b, i: (b * per_seq + i, 0))
    prev = pl.BlockSpec(
        (blk, ATT_KV), lambda b, i: (b * (S // blk) + jnp.maximum(i * ratio - 1, 0), 0))
    return pl.pallas_call(
        _mixer_kernel,
        grid=(B, per_seq),
        in_specs=[pl.BlockSpec(memory_space=pltpu.SMEM), cur(ATT_Q),
                  cur(ATT_KV), cur(ATT_KV), cur(ATT_KV), cur(ATT_KV),
                  prev, prev, prev, prev,
                  _const_spec((ATT_HEADS, blk, 2 * blk)),
                  cur(HG_K), cur(HG_K), cur(HG_K), cur(HG_K), cur(HG_V), cur(HG_V),
                  _const_spec((1, HG_DV)), _const_spec(tri.shape), _const_spec(masks.shape),
                  cur(D_MODEL), cur(D_MODEL), cur(D_MODEL),
                  _const_spec((ATT_Q, D_MODEL)), _const_spec((HG_V, D_MODEL)),
                  _const_spec((D_MODEL, D_MODEL)), _const_spec((1, D_MODEL))],
        out_specs=[cur(D_MODEL), cur(D_MODEL)],
        out_shape=[jax.ShapeDtypeStruct((T, D_MODEL), F32),
                   jax.ShapeDtypeStruct((T, D_MODEL), BF16)],
        scratch_shapes=[pltpu.VMEM((tm + blk, ATT_KV), BF16)] * 4
        + [pltpu.VMEM((tm, ATT_Q), BF16), pltpu.VMEM((tm, HG_V), BF16),
           pltpu.VMEM((HG_HEADS, HG_DV, HG_DK), F32)],
        compiler_params=_params(),
        name="mixer",
    )(sinks, qa, ka, kb, va, vb, ka, kb, va, vb, bias_tab,
      hq, hk, ghi, glo, hv, hgate, hn, tri, masks, sga, sgr, xf, wpa, wph, wout, n2)


def _ffn_kernel(h2_ref, x1_ref, wup_ref, cw_ref, cb_ref, wdn_ref, o_ref,
                carry_ref, act_ref, *, tiles_per_seq):
    tm = h2_ref.shape[0]
    halo = carry_ref.shape[0]

    @pl.when(pl.program_id(0) % tiles_per_seq == 0)
    def _():
        carry_ref[...] = jnp.zeros_like(carry_ref)

    h2 = h2_ref[...]
    head_row = lax.broadcasted_iota(jnp.int32, (halo, FFN_CW), 0)

    def shifted(u, prev, n):
        s = pltpu.roll(u, n, 0)
        head = jnp.where(head_row < n, pltpu.roll(prev, n, 0), s[0:halo, :])
        return jnp.concatenate([head, s[halo:, :]], axis=0)

    def conv(off):
        cols = slice(off, off + FFN_CW)
        u = _dot(h2, wup_ref[:, cols])
        prev = carry_ref[:, cols]
        carry_ref[:, cols] = u[tm - halo:tm, :]
        w = cw_ref[:, cols]
        return (w[0:1, :] * shifted(u, prev, 2) + w[1:2, :] * shifted(u, prev, 1)
                + w[2:3, :] * u + cb_ref[:, cols])

    for c in range(D_FF // FFN_CW):
        ug = conv(c * FFN_CW)
        uv = conv(D_FF + c * FFN_CW)
        act_ref[:, c * FFN_CW:(c + 1) * FFN_CW] = (ug * _sigmoid(ug) * uv).astype(BF16)
    o_ref[...] = x1_ref[...] + _dot(act_ref[...], wdn_ref[...])


def _ffn(h2, x1, wup, cw, cb, wdn, S):
    T = h2.shape[0]
    tm = TM_FFN
    halo = 8
    row = pl.BlockSpec((tm, D_MODEL), lambda i: (i, 0))
    return pl.pallas_call(
        functools.partial(_ffn_kernel, tiles_per_seq=S // tm),
        grid=(T // tm,),
        in_specs=[row, row, _const_spec((D_MODEL, 2 * D_FF)),
                  _const_spec((CONV_WIDTH, 2 * D_FF)), _const_spec((1, 2 * D_FF)),
                  _const_spec((D_FF, D_MODEL))],
        out_specs=row,
        out_shape=jax.ShapeDtypeStruct((T, D_MODEL), F32),
        scratch_shapes=[pltpu.VMEM((halo, 2 * D_FF), F32),
                        pltpu.VMEM((tm, D_FF), BF16)],
        compiler_params=_params(),
        name="conv_ffn",
    )(h2, x1, wup, cw, cb, wdn)


def _bias_table(rel_bias):
    blk = ATT_BLOCK
    assert WINDOW == blk
    d = np.arange(WINDOW)
    max_exact = N_BUCKETS // 2
    large = max_exact + (np.log(np.maximum(d, 1) / max_exact) / math.log(MAX_DISTANCE / max_exact)
                         * (N_BUCKETS - max_exact)).astype(np.int32)
    bucket = np.where(d < max_exact, d, np.minimum(large, N_BUCKETS - 1))
    onehot = jnp.asarray(bucket[:, None] == np.arange(N_BUCKETS)[None, :], F32)
    per_dist = jnp.dot(onehot, rel_bias.astype(F32), precision=lax.Precision.HIGHEST)
    heads = per_dist.shape[1]
    period = 3 * blk
    z = jnp.concatenate([jnp.full((heads, 1), -jnp.inf, F32), per_dist[::-1].T,
                         jnp.full((heads, period - WINDOW - 1), -jnp.inf, F32)], axis=1)
    skew = jnp.tile(z, (1, blk))[:, :blk * (period - 1)].reshape(heads, blk, period - 1)
    return skew[:, :, :2 * blk]


def _block_diag_ones(n, blk):
    idx = np.arange(n) // blk
    return jnp.asarray((idx[:, None] == idx[None, :]).astype(np.float32), BF16)


def kernel(x, norm1, w_in, q_norm, k_norm, sinks, rel_bias, hg_lb, hg_norm,
           w_pa, w_ph, w_out, norm2, w_up, conv_w, conv_b, w_down):
    B, S, D = x.shape
    T = B * S
    depth = w_in.shape[0]
    assert D == D_MODEL and T % TM_PROJ == 0 and S % TM_FFN == 0 and S % TM_MIX == 0
    assert TM_MIX % HG_BLOCK == 0 and HG_BLOCK % (2 * ATT_BLOCK) == 0

    p_lb = jax.nn.softmax(hg_lb.astype(F32), axis=0)
    lbs = jnp.cumsum(p_lb, axis=0) - p_lb[0:1]

    bias_tab = _bias_table(rel_bias)
    bdq = _block_diag_ones(ATT_Q, ATT_HEAD_DIM)
    bdk = _block_diag_ones(ATT_KV, ATT_HEAD_DIM)
    scale = ATT_HEAD_DIM ** -0.5

    xf = x.reshape(T, D)
    for l in range(depth):
        qg = (jnp.tile(q_norm[l].astype(F32), ATT_HEADS) * scale).reshape(1, ATT_Q)
        kg = jnp.tile(k_norm[l].astype(F32), ATT_KV_HEADS).reshape(1, ATT_KV)
        (qa, ka, kb, va, vb, hq, hk, ghi, glo, hv, hgate, sga, sgr) = _inproj(
            xf, norm1[l].reshape(1, D), w_in[l].astype(BF16), qg, kg, bdq, bdk,
            lbs[l].reshape(1, HG_K))
        x1, h2 = _mixer(sinks[l].astype(F32), qa, ka, kb, va, vb, bias_tab,
                        hq, hk, ghi, glo, hv, hgate, hg_norm[l].reshape(1, HG_DV).astype(F32),
                        sga, sgr, xf, w_pa[l].astype(BF16), w_ph[l].astype(BF16),
                        w_out[l].astype(BF16), norm2[l].reshape(1, D), B, S)
        xf = _ffn(h2, x1, w_up[l].astype(BF16), conv_w[l].astype(F32),
                  conv_b[l].reshape(1, 2 * D_FF).astype(F32), w_down[l].astype(BF16), S)
    return xf.reshape(B, S, D)
```

```python
import functools
import math

import numpy as np
import jax
import jax.numpy as jnp
from jax import lax
from jax.experimental import pallas as pl
from jax.experimental.pallas import tpu as pltpu

F32 = jnp.float32
BF16 = jnp.bfloat16

D_MODEL = 1024
ATT_HEADS = 8
ATT_KV_HEADS = 2
ATT_HEAD_DIM = 64
WINDOW = 128
ATT_BLOCK = 128
N_BUCKETS = 32
MAX_DISTANCE = 128
HG_HEADS = 4
HG_DK = 128
HG_DV = 128
HG_CHUNK = 64
D_FF = 2816
CONV_WIDTH = 3
EPS = 1e-6

ATT_Q = ATT_HEADS * ATT_HEAD_DIM
ATT_KV = ATT_KV_HEADS * ATT_HEAD_DIM
HG_K = HG_HEADS * HG_DK
HG_V = HG_HEADS * HG_DV
IN_SPLITS = (ATT_Q, ATT_KV, ATT_KV, HG_K, HG_K, HG_V, HG_V, D_MODEL, D_MODEL)
IN_OFFS = tuple(int(o) for o in np.cumsum((0,) + IN_SPLITS))
D_IN = IN_OFFS[-1]

LANES = 128
VMEM_LIMIT_BYTES = 56 * 1024 * 1024

TM_PROJ = 512
TM_FFN = 512
FFN_CW = 256
TM_MIX = 512
HG_BLOCK = 256
HG_LEVELS = (32, 16, 8, 4, 2, 1)
HG_SAFE_SPAN = 80.0


def _dot(a, b):
    return jnp.dot(a, b, preferred_element_type=F32)


def _dot_nt(a, b):
    return lax.dot_general(a, b, (((1,), (1,)), ((), ())), preferred_element_type=F32)


def _dot_tn(a, b):
    return lax.dot_general(a, b, (((0,), (0,)), ((), ())), preferred_element_type=F32)


def _sigmoid(x):
    return 1.0 / (1.0 + jnp.exp(-x))


def _rms_scale(x):
    return lax.rsqrt(jnp.mean(x * x, axis=-1, keepdims=True) + EPS)


def _const_spec(shape):
    nd = len(shape)
    return pl.BlockSpec(shape, lambda *_: (0,) * nd)


def _params():
    return pltpu.CompilerParams(vmem_limit_bytes=VMEM_LIMIT_BYTES)


def _inproj_kernel(x_ref, n1_ref, w_ref, qg_ref, kg_ref, bdq_ref, bdk_ref, lb_ref, hn_ref,
                   qa_ref, ka_ref, kb_ref, va_ref, vb_ref,
                   hq_ref, hk_ref, ghi_ref, glo_ref, hv_ref, hgate_ref, sga_ref, sgr_ref):
    x = x_ref[...]
    h = (x * _rms_scale(x) * n1_ref[...]).astype(BF16)

    def proj(seg):
        return _dot(h, w_ref[:, IN_OFFS[seg]:IN_OFFS[seg + 1]])

    q = proj(0)
    q_ms = _dot((q * q).astype(BF16), bdq_ref[...]) * (1.0 / ATT_HEAD_DIM)
    qa_ref[...] = (q * lax.rsqrt(q_ms + EPS) * qg_ref[...]).astype(BF16)

    k = proj(1)
    k_ms = _dot((k * k).astype(BF16), bdk_ref[...]) * (1.0 / ATT_HEAD_DIM)
    kn = k * lax.rsqrt(k_ms + EPS) * kg_ref[...]
    ka_ref[...] = kn.astype(BF16)
    kb_ref[...] = pltpu.roll(kn, ATT_HEAD_DIM, 1).astype(BF16)

    v = proj(2)
    va_ref[...] = v.astype(BF16)
    vb_ref[...] = pltpu.roll(v, ATT_HEAD_DIM, 1).astype(BF16)

    qr = proj(3)
    hq_ref[...] = (qr * _sigmoid(qr)).astype(BF16)
    lb = lb_ref[...]
    f = lb + (1.0 - lb) * _sigmoid(proj(4))
    hk_ref[...] = (1.0 - f).astype(BF16)
    g = jnp.log2(f)
    g_hi = g.astype(BF16)
    ghi_ref[...] = g_hi
    glo_ref[...] = (g - g_hi.astype(F32)).astype(BF16)
    hv_ref[...] = proj(5).astype(BF16)
    gr = proj(6)
    hgate_ref[...] = (gr * _sigmoid(gr) * hn_ref[...]).astype(BF16)
    sga_ref[...] = _sigmoid(proj(7)).astype(BF16)
    sgr_ref[...] = _sigmoid(proj(8)).astype(BF16)


def _inproj(xf, n1, w_in, qg, kg, bdq, bdk, lb, hn):
    T = xf.shape[0]
    tm = TM_PROJ
    row = lambda n: pl.BlockSpec((tm, n), lambda i: (i, 0))
    out_widths = [ATT_Q, ATT_KV, ATT_KV, ATT_KV, ATT_KV,
                  HG_K, HG_K, HG_K, HG_K, HG_V, HG_V, D_MODEL, D_MODEL]
    return pl.pallas_call(
        _inproj_kernel,
        grid=(T // tm,),
        in_specs=[row(D_MODEL), _const_spec((1, D_MODEL)), _const_spec((D_MODEL, D_IN)),
                  _const_spec((1, ATT_Q)), _const_spec((1, ATT_KV)),
                  _const_spec((ATT_Q, ATT_Q)), _const_spec((ATT_KV, ATT_KV)),
                  _const_spec((1, HG_K)), _const_spec((1, HG_V))],
        out_specs=[row(n) for n in out_widths],
        out_shape=[jax.ShapeDtypeStruct((T, n), BF16) for n in out_widths],
        compiler_params=_params(),
        name="inproj",
    )(xf, n1, w_in, qg, kg, bdq, bdk, lb, hn)


def _hgrn_masks():
    n = HG_BLOCK // 2
    t = np.arange(n)[:, None]
    s = np.arange(n)[None, :]

    def level(m):
        return ((t // (2 * m)) == (s // (2 * m))) & ((t // m) % 2 == 1) & ((s // m) % 2 == 0)

    masks = [((t // HG_CHUNK) == (s // HG_CHUNK)) & (s <= t)]
    masks += [level(m) for m in HG_LEVELS] + [t == s]
    tri = np.tril(np.ones((HG_BLOCK, HG_BLOCK), np.float32))
    return np.stack(masks, axis=0).astype(np.float32), np.concatenate([tri, tri], axis=1)


def _attention_rows(r0, n_rows, first_in_seq, sink_ref, q_ref, bias_ref, bands, a_ref):
    blk = ATT_BLOCK
    ka_s, kb_s, va_s, vb_s = bands
    lane_q = lax.broadcasted_iota(jnp.int32, (blk, LANES), 1)
    lane_kv = lax.broadcasted_iota(jnp.int32, (2 * blk, LANES), 1)
    col = lax.broadcasted_iota(jnp.int32, (blk, 2 * blk), 1)
    lo_q = lane_q < ATT_HEAD_DIM
    q_lo = jnp.where(lo_q, 1.0, 0.0).astype(BF16)
    q_hi = jnp.where(lo_q, 0.0, 1.0).astype(BF16)
    ind_lo = jnp.where(lane_kv < ATT_HEAD_DIM, 1.0, 0.0).astype(BF16)
    ind_hi = jnp.where(lane_kv < ATT_HEAD_DIM, 0.0, 1.0).astype(BF16)
    no_prev = jnp.where(first_in_seq, -jnp.inf, 0.0).astype(F32)

    for q0 in range(r0, r0 + n_rows, blk):
        band = {"a": (ka_s[q0:q0 + 2 * blk, :], va_s[q0:q0 + 2 * blk, :]),
                "b": (kb_s[q0:q0 + 2 * blk, :], vb_s[q0:q0 + 2 * blk, :])}
        for g in range(ATT_HEADS // 2):
            even_src, odd_src = ("a", "b") if g < 2 else ("b", "a")
            q2 = q_ref[q0:q0 + blk, g * LANES:(g + 1) * LANES]
            probs, sink_terms = [], []
            for half, src in ((0, even_src), (1, odd_src)):
                head = 2 * g + half
                qh = q2 * (q_lo if half == 0 else q_hi)
                s = _dot_nt(qh, band[src][0]) + bias_ref[head]
                if q0 == 0:
                    s = s + jnp.where(col < blk, no_prev, 0.0)
                sink = sink_ref[head]
                m = jnp.maximum(jnp.max(s, axis=-1, keepdims=True), sink)
                probs.append(jnp.exp(s - m).astype(BF16))
                sink_terms.append(jnp.exp(sink - m))
            p_cat = jnp.concatenate(probs, axis=1)
            v_even = band[even_src][1] * ind_lo
            v_odd = band[odd_src][1] * ind_hi
            v_stack = jnp.concatenate(
                [jnp.concatenate([v_even, ind_lo], axis=1),
                 jnp.concatenate([v_odd, ind_hi], axis=1)], axis=0)
            res = _dot(p_cat, v_stack)
            den = res[:, LANES:] + jnp.where(lo_q, sink_terms[0], sink_terms[1])
            a_ref[q0:q0 + blk, g * LANES:(g + 1) * LANES] = (res[:, :LANES] / den).astype(BF16)


def _hgrn_block(r0, q_ref, k_ref, ghi_ref, glo_ref, v_ref, gate_ref, tri2, mask_ref,
                st_ref, r_ref):
    N = HG_BLOCK
    H = N // 2
    C = HG_CHUNK
    W = q_ref.shape[1]
    rows = slice(r0, r0 + N)
    heads = [slice(h * HG_DK, (h + 1) * HG_DK) for h in range(HG_HEADS)]
    halves = [slice(0, H), slice(H, N)]
    qq = q_ref[rows, :]
    kk = k_ref[rows, :]
    vv = v_ref[rows, :]

    g_pieces = jnp.concatenate([ghi_ref[rows, :], glo_ref[rows, :]], axis=0)
    b = _dot(tri2, g_pieces)
    b_last = b[N - 1:N, :]
    decay = jnp.exp2(b_last)

    def neg_offset(m):
        parts = []
        for k in range(0, N, 2 * m):
            ref = b[k + m - 1:k + m, :]
            parts += [ref - b[k:k + m, :], b[k + m:k + 2 * m, :] - ref]
        return jnp.concatenate(parts, axis=0)

    def finish(h, att_top, att_bot, att_cross, o_state):
        sl = heads[h]
        o_top = _dot(att_top.astype(BF16), vv[0:H, sl])
        o_bot = _dot(jnp.concatenate([att_cross, att_bot], axis=1).astype(BF16), vv[:, sl])
        o = jnp.concatenate([o_top, o_bot], axis=0) + o_state
        r = o * _rms_scale(o)
        r_ref[rows, sl] = (r * gate_ref[rows, sl].astype(F32)).astype(BF16)

    q_state = qq * jnp.exp2(b).astype(BF16)
    k_state = kk * jnp.exp2(b_last - b).astype(BF16)

    e128 = jnp.exp2(neg_offset(H)).astype(BF16)
    q128 = qq[H:N, :] * e128[H:N, :]
    k128 = kk[0:H, :] * e128[0:H, :]
    e64 = jnp.exp2(neg_offset(C)).astype(BF16)
    q64 = [qq[r + C:r + 2 * C, :] * e64[r + C:r + 2 * C, :] for r in (0, H)]
    k64 = [kk[r:r + C, :] * e64[r:r + C, :] for r in (0, H)]

    so_mid = neg_offset(C // 2)
    span = -jnp.min(so_mid)
    e_mid = jnp.exp2(so_mid)
    small = e_mid.astype(BF16)
    large = (1.0 / e_mid).astype(BF16)
    q_sc, k_sc = [], []
    for k in range(0, N, C):
        q_sc += [large[k:k + C // 2, :], small[k + C // 2:k + C, :]]
        k_sc += [small[k:k + C // 2, :], large[k + C // 2:k + C, :]]
    q_mid = qq * jnp.concatenate(q_sc, axis=0)
    k_mid = kk * jnp.concatenate(k_sc, axis=0)
    in_chunk = mask_ref[0] > 0.5
    pad = jnp.zeros((C, H - C), F32)

    def with_level64(fine, p64):
        return jnp.concatenate(
            [fine[0:C, :], fine[C:H, :] + jnp.concatenate([p64, pad], axis=1)], axis=0)

    cross, lvl64, o_state = [], [], []
    for h, sl in enumerate(heads):
        cross.append(_dot_nt(q128[:, sl], k128[:, sl]))
        lvl64.append([_dot_nt(q64[i][:, sl], k64[i][:, sl]) for i in range(2)])
        st = st_ref[h]
        o_state.append(_dot_nt(q_state[:, sl], st.astype(BF16)))
        st_ref[h] = st * decay[:, sl] + _dot_tn(vv[:, sl], k_state[:, sl])
        att = [with_level64(jnp.where(in_chunk, _dot_nt(q_mid[hv, sl], k_mid[hv, sl]), 0.0),
                            lvl64[h][i]) for i, hv in enumerate(halves)]
        finish(h, att[0], att[1], cross[h], o_state[h])

    @pl.when(span >= HG_SAFE_SPAN)
    def _():
        def offset(m):
            ref = [jnp.broadcast_to(b[k + m - 1:k + m, :], (2 * m, W)) for k in range(0, N, 2 * m)]
            return b - jnp.concatenate(ref, axis=0)

        f = jnp.exp2(ghi_ref[rows, :].astype(F32) + glo_ref[rows, :].astype(F32))
        scales = [e_mid]
        scales += [jnp.exp2(-jnp.abs(offset(m))) for m in HG_LEVELS[1:] if 2 * m >= 8]
        row = lax.broadcasted_iota(jnp.int32, (N, W), 0)
        f_next = pltpu.roll(f, N - 1, 0)
        f_prev = pltpu.roll(f, 1, 0)
        r4 = row & 3
        scales.append(jnp.where(r4 == 0, f_next,
                                jnp.where(r4 == 1, 1.0, jnp.where(r4 == 2, f, f * f_prev))))
        scales.append(jnp.where((row & 1) == 1, f, 1.0))
        pairs = [(qq * e.astype(BF16), kk * e.astype(BF16)) for e in scales] + [(qq, kk)]
        for h, sl in enumerate(heads):
            att = []
            for i, hv in enumerate(halves):
                fine = jnp.zeros((H, H), F32)
                for j, (ql, kl) in enumerate(pairs):
                    fine = fine + _dot_nt(ql[hv, sl], kl[hv, sl]) * mask_ref[1 + j]
                att.append(with_level64(fine, lvl64[h][i]))
            finish(h, att[0], att[1], cross[h], o_state[h])


def _merge_rows(rows, a_ref, r_ref, sga_ref, sgr_ref, x_ref, wpa_ref, wph_ref, wout_ref, n2,
                x1_ref, h2_ref):
    pa = _dot(a_ref[rows, :], wpa_ref[...])
    ph = _dot(r_ref[rows, :], wph_ref[...])
    merged = sga_ref[rows, :].astype(F32) * pa + sgr_ref[rows, :].astype(F32) * ph
    x1 = x_ref[rows, :] + _dot(merged.astype(BF16), wout_ref[...])
    x1_ref[rows, :] = x1
    h2_ref[rows, :] = (x1 * _rms_scale(x1) * n2).astype(BF16)


def _mixer_kernel(sink_ref, qa_ref, kac_ref, kbc_ref, vac_ref, vbc_ref,
                  kap_ref, kbp_ref, vap_ref, vbp_ref, bias_ref,
                  hq_ref, hk_ref, ghi_ref, glo_ref, hv_ref, hgate_ref, tri_ref, mask_ref,
                  sga_ref, sgr_ref, x_ref, wpa_ref, wph_ref, wout_ref, n2_ref,
                  x1_ref, h2_ref,
                  ka_s, kb_s, va_s, vb_s, a_s, r_s, st_ref):
    blk = ATT_BLOCK
    tile = qa_ref.shape[0]
    first_in_seq = pl.program_id(1) == 0

    @pl.when(first_in_seq)
    def _():
        st_ref[...] = jnp.zeros_like(st_ref)

    for s_ref, p_ref, c_ref in ((ka_s, kap_ref, kac_ref), (kb_s, kbp_ref, kbc_ref),
                                (va_s, vap_ref, vac_ref), (vb_s, vbp_ref, vbc_ref)):
        s_ref[0:blk, :] = p_ref[...]
        s_ref[blk:, :] = c_ref[...]

    tri = tri_ref[...]
    n2 = n2_ref[...]
    for r0 in range(0, tile, HG_BLOCK):
        _attention_rows(r0, HG_BLOCK, first_in_seq, sink_ref, qa_ref, bias_ref,
                        (ka_s, kb_s, va_s, vb_s), a_s)
        _hgrn_block(r0, hq_ref, hk_ref, ghi_ref, glo_ref, hv_ref, hgate_ref, tri, mask_ref,
                    st_ref, r_s)
        _merge_rows(slice(r0, r0 + HG_BLOCK), a_s, r_s, sga_ref, sgr_ref, x_ref,
                    wpa_ref, wph_ref, wout_ref, n2, x1_ref, h2_ref)


def _mixer(sinks, qa, ka, kb, va, vb, bias_tab, hq, hk, ghi, glo, hv, hgate,
           sga, sgr, xf, wpa, wph, wout, n2, B, S):
    T = qa.shape[0]
    tm = TM_MIX
    blk = ATT_BLOCK
    per_seq = S // tm
    ratio = tm // blk
    masks_np, tri_np = _hgrn_masks()
    masks = jnp.asarray(masks_np, F32)
    tri = jnp.asarray(tri_np, BF16)
    cur = lambda n: pl.BlockSpec((tm, n), lambda b, i: (b * per_seq + i, 0))
    prev = pl.BlockSpec(
        (blk, ATT_KV), lambda b, i: (b * (S // blk) + jnp.maximum(i * ratio - 1, 0), 0))
    return pl.pallas_call(
        _mixer_kernel,
        grid=(B, per_seq),
        in_specs=[pl.BlockSpec(memory_space=pltpu.SMEM), cur(ATT_Q),
                  cur(ATT_KV), cur(ATT_KV), cur(ATT_KV), cur(ATT_KV),
                  prev, prev, prev, prev,
                  _const_spec((ATT_HEADS, blk, 2 * blk)),
                  cur(HG_K), cur(HG_K), cur(HG_K), cur(HG_K), cur(HG_V), cur(HG_V),
                  _const_spec(tri.shape), _const_spec(masks.shape),
                  cur(D_MODEL), cur(D_MODEL), cur(D_MODEL),
                  _const_spec((ATT_Q, D_MODEL)), _const_spec((HG_V, D_MODEL)),
                  _const_spec((D_MODEL, D_MODEL)), _const_spec((1, D_MODEL))],
        out_specs=[cur(D_MODEL), cur(D_MODEL)],
        out_shape=[jax.ShapeDtypeStruct((T, D_MODEL), F32),
                   jax.ShapeDtypeStruct((T, D_MODEL), BF16)],
        scratch_shapes=[pltpu.VMEM((tm + blk, ATT_KV), BF16)] * 4
        + [pltpu.VMEM((tm, ATT_Q), BF16), pltpu.VMEM((tm, HG_V), BF16),
           pltpu.VMEM((HG_HEADS, HG_DV, HG_DK), F32)],
        compiler_params=_params(),
        name="mixer",
    )(sinks, qa, ka, kb, va, vb, ka, kb, va, vb, bias_tab,
      hq, hk, ghi, glo, hv, hgate, tri, masks, sga, sgr, xf, wpa, wph, wout, n2)


def _ffn_kernel(h2_ref, x1_ref, wup_ref, cw_ref, cb_ref, wdn_ref, o_ref,
                carry_ref, act_ref, *, tiles_per_seq):
    tm = h2_ref.shape[0]
    halo = carry_ref.shape[0]

    @pl.when(pl.program_id(0) % tiles_per_seq == 0)
    def _():
        carry_ref[...] = jnp.zeros_like(carry_ref)

    h2 = h2_ref[...]
    head_row = lax.broadcasted_iota(jnp.int32, (halo, FFN_CW), 0)

    def shifted(u, prev, n):
        s = pltpu.roll(u, n, 0)
        head = jnp.where(head_row < n, pltpu.roll(prev, n, 0), s[0:halo, :])
        return jnp.concatenate([head, s[halo:, :]], axis=0)

    def conv(off):
        cols = slice(off, off + FFN_CW)
        u = _dot(h2, wup_ref[:, cols])
        prev = carry_ref[:, cols]
        carry_ref[:, cols] = u[tm - halo:tm, :]
        w = cw_ref[:, cols]
        return (w[0:1, :] * shifted(u, prev, 2) + w[1:2, :] * shifted(u, prev, 1)
                + w[2:3, :] * u + cb_ref[:, cols])

    for c in range(D_FF // FFN_CW):
        ug = conv(c * FFN_CW)
        uv = conv(D_FF + c * FFN_CW)
        act_ref[:, c * FFN_CW:(c + 1) * FFN_CW] = (ug * _sigmoid(ug) * uv).astype(BF16)
    o_ref[...] = x1_ref[...] + _dot(act_ref[...], wdn_ref[...])


def _ffn(h2, x1, wup, cw, cb, wdn, S):
    T = h2.shape[0]
    tm = TM_FFN
    halo = 8
    row = pl.BlockSpec((tm, D_MODEL), lambda i: (i, 0))
    return pl.pallas_call(
        functools.partial(_ffn_kernel, tiles_per_seq=S // tm),
        grid=(T // tm,),
        in_specs=[row, row, _const_spec((D_MODEL, 2 * D_FF)),
                  _const_spec((CONV_WIDTH, 2 * D_FF)), _const_spec((1, 2 * D_FF)),
                  _const_spec((D_FF, D_MODEL))],
        out_specs=row,
        out_shape=jax.ShapeDtypeStruct((T, D_MODEL), F32),
        scratch_shapes=[pltpu.VMEM((halo, 2 * D_FF), F32),
                        pltpu.VMEM((tm, D_FF), BF16)],
        compiler_params=_params(),
        name="conv_ffn",
    )(h2, x1, wup, cw, cb, wdn)


def _bias_table(rel_bias):
    blk = ATT_BLOCK
    assert WINDOW == blk
    d = np.arange(WINDOW)
    max_exact = N_BUCKETS // 2
    large = max_exact + (np.log(np.maximum(d, 1) / max_exact) / math.log(MAX_DISTANCE / max_exact)
                         * (N_BUCKETS - max_exact)).astype(np.int32)
    bucket = np.where(d < max_exact, d, np.minimum(large, N_BUCKETS - 1))
    onehot = jnp.asarray(bucket[:, None] == np.arange(N_BUCKETS)[None, :], F32)
    per_dist = jnp.dot(onehot, rel_bias.astype(F32), precision=lax.Precision.HIGHEST)
    heads = per_dist.shape[1]
    period = 3 * blk
    z = jnp.concatenate([jnp.full((heads, 1), -jnp.inf, F32), per_dist[::-1].T,
                         jnp.full((heads, period - WINDOW - 1), -jnp.inf, F32)], axis=1)
    skew = jnp.tile(z, (1, blk))[:, :blk * (period - 1)].reshape(heads, blk, period - 1)
    return skew[:, :, :2 * blk]


def _block_diag_ones(n, blk):
    idx = np.arange(n) // blk
    return jnp.asarray((idx[:, None] == idx[None, :]).astype(np.float32), BF16)


def kernel(x, norm1, w_in, q_norm, k_norm, sinks, rel_bias, hg_lb, hg_norm,
           w_pa, w_ph, w_out, norm2, w_up, conv_w, conv_b, w_down):
    B, S, D = x.shape
    T = B * S
    depth = w_in.shape[0]
    assert D == D_MODEL and T % TM_PROJ == 0 and S % TM_FFN == 0 and S % TM_MIX == 0
    assert TM_MIX % HG_BLOCK == 0 and HG_BLOCK % (2 * ATT_BLOCK) == 0

    p_lb = jax.nn.softmax(hg_lb.astype(F32), axis=0)
    lbs = jnp.cumsum(p_lb, axis=0) - p_lb[0:1]

    bias_tab = _bias_table(rel_bias)
    bdq = _block_diag_ones(ATT_Q, ATT_HEAD_DIM)
    bdk = _block_diag_ones(ATT_KV, ATT_HEAD_DIM)
    scale = ATT_HEAD_DIM ** -0.5

    xf = x.reshape(T, D)
    for l in range(depth):
        qg = (jnp.tile(q_norm[l].astype(F32), ATT_HEADS) * scale).reshape(1, ATT_Q)
        kg = jnp.tile(k_norm[l].astype(F32), ATT_KV_HEADS).reshape(1, ATT_KV)
        (qa, ka, kb, va, vb, hq, hk, ghi, glo, hv, hgate, sga, sgr) = _inproj(
            xf, norm1[l].reshape(1, D), w_in[l].astype(BF16), qg, kg, bdq, bdk,
            lbs[l].reshape(1, HG_K),
            jnp.tile(hg_norm[l].astype(F32), HG_HEADS).reshape(1, HG_V))
        x1, h2 = _mixer(sinks[l].astype(F32), qa, ka, kb, va, vb, bias_tab,
                        hq, hk, ghi, glo, hv, hgate,
                        sga, sgr, xf, w_pa[l].astype(BF16), w_ph[l].astype(BF16),
                        w_out[l].astype(BF16), norm2[l].reshape(1, D), B, S)
        xf = _ffn(h2, x1, w_up[l].astype(BF16), conv_w[l].astype(F32),
                  conv_b[l].reshape(1, 2 * D_FF).astype(F32), w_down[l].astype(BF16), S)
    return xf.reshape(B, S, D)
```

```python
import functools
import math

import numpy as np
import jax
import jax.numpy as jnp
from jax import lax
from jax.experimental import pallas as pl
from jax.experimental.pallas import tpu as pltpu

F32 = jnp.float32
BF16 = jnp.bfloat16

D_MODEL = 1024
ATT_HEADS = 8
ATT_KV_HEADS = 2
ATT_HEAD_DIM = 64
WINDOW = 128
ATT_BLOCK = 128
N_BUCKETS = 32
MAX_DISTANCE = 128
HG_HEADS = 4
HG_DK = 128
HG_DV = 128
HG_CHUNK = 64
D_FF = 2816
CONV_WIDTH = 3
EPS = 1e-6

ATT_Q = ATT_HEADS * ATT_HEAD_DIM
ATT_KV = ATT_KV_HEADS * ATT_HEAD_DIM
HG_K = HG_HEADS * HG_DK
HG_V = HG_HEADS * HG_DV
IN_SPLITS = (ATT_Q, ATT_KV, ATT_KV, HG_K, HG_K, HG_V, HG_V, D_MODEL, D_MODEL)
IN_OFFS = tuple(int(o) for o in np.cumsum((0,) + IN_SPLITS))
D_IN = IN_OFFS[-1]

LANES = 128
VMEM_LIMIT_BYTES = 56 * 1024 * 1024

TM_PROJ = 512
TM_FFN = 512
FFN_CW = 256
TM_MIX = 512
MERGE_CW = 256
HG_BLOCK = 256
HG_LEVELS = (32, 16, 8, 4, 2, 1)
HG_SAFE_SPAN = 80.0


def _dot(a, b):
    return jnp.dot(a, b, preferred_element_type=F32)


def _dot_nt(a, b):
    return lax.dot_general(a, b, (((1,), (1,)), ((), ())), preferred_element_type=F32)


def _dot_tn(a, b):
    return lax.dot_general(a, b, (((0,), (0,)), ((), ())), preferred_element_type=F32)


def _sigmoid(x):
    return 1.0 / (1.0 + jnp.exp(-x))


def _rms_scale(x):
    return lax.rsqrt(jnp.mean(x * x, axis=-1, keepdims=True) + EPS)


def _const_spec(shape):
    nd = len(shape)
    return pl.BlockSpec(shape, lambda *_: (0,) * nd)


def _params():
    return pltpu.CompilerParams(vmem_limit_bytes=VMEM_LIMIT_BYTES)


def _inproj_kernel(x_ref, n1_ref, w_ref, qg_ref, kg_ref, bdq_ref, bdk_ref, lb_ref, hn_ref,
                   qa_ref, ka_ref, kb_ref, va_ref, vb_ref,
                   hq_ref, hk_ref, ghi_ref, glo_ref, hv_ref, hgate_ref, sga_ref, sgr_ref):
    x = x_ref[...]
    h = (x * _rms_scale(x) * n1_ref[...]).astype(BF16)

    def proj(seg):
        return _dot(h, w_ref[:, IN_OFFS[seg]:IN_OFFS[seg + 1]])

    q = proj(0)
    q_ms = _dot((q * q).astype(BF16), bdq_ref[...]) * (1.0 / ATT_HEAD_DIM)
    qa_ref[...] = (q * lax.rsqrt(q_ms + EPS) * qg_ref[...]).astype(BF16)

    k = proj(1)
    k_ms = _dot((k * k).astype(BF16), bdk_ref[...]) * (1.0 / ATT_HEAD_DIM)
    kn = k * lax.rsqrt(k_ms + EPS) * kg_ref[...]
    ka_ref[...] = kn.astype(BF16)
    kb_ref[...] = pltpu.roll(kn, ATT_HEAD_DIM, 1).astype(BF16)

    v = proj(2)
    va_ref[...] = v.astype(BF16)
    vb_ref[...] = pltpu.roll(v, ATT_HEAD_DIM, 1).astype(BF16)

    qr = proj(3)
    hq_ref[...] = (qr * _sigmoid(qr)).astype(BF16)
    lb = lb_ref[...]
    f = lb + (1.0 - lb) * _sigmoid(proj(4))
    hk_ref[...] = (1.0 - f).astype(BF16)
    g = jnp.log2(f)
    g_hi = g.astype(BF16)
    ghi_ref[...] = g_hi
    glo_ref[...] = (g - g_hi.astype(F32)).astype(BF16)
    hv_ref[...] = proj(5).astype(BF16)
    gr = proj(6)
    hgate_ref[...] = (gr * _sigmoid(gr) * hn_ref[...]).astype(BF16)
    sga_ref[...] = _sigmoid(proj(7)).astype(BF16)
    sgr_ref[...] = _sigmoid(proj(8)).astype(BF16)


def _inproj(xf, n1, w_in, qg, kg, bdq, bdk, lb, hn):
    T = xf.shape[0]
    tm = TM_PROJ
    row = lambda n: pl.BlockSpec((tm, n), lambda i: (i, 0))
    out_widths = [ATT_Q, ATT_KV, ATT_KV, ATT_KV, ATT_KV,
                  HG_K, HG_K, HG_K, HG_K, HG_V, HG_V, D_MODEL, D_MODEL]
    return pl.pallas_call(
        _inproj_kernel,
        grid=(T // tm,),
        in_specs=[row(D_MODEL), _const_spec((1, D_MODEL)), _const_spec((D_MODEL, D_IN)),
                  _const_spec((1, ATT_Q)), _const_spec((1, ATT_KV)),
                  _const_spec((ATT_Q, ATT_Q)), _const_spec((ATT_KV, ATT_KV)),
                  _const_spec((1, HG_K)), _const_spec((1, HG_V))],
        out_specs=[row(n) for n in out_widths],
        out_shape=[jax.ShapeDtypeStruct((T, n), BF16) for n in out_widths],
        compiler_params=_params(),
        name="inproj",
    )(xf, n1, w_in, qg, kg, bdq, bdk, lb, hn)


def _hgrn_masks():
    n = HG_BLOCK // 2
    t = np.arange(n)[:, None]
    s = np.arange(n)[None, :]

    def level(m):
        return ((t // (2 * m)) == (s // (2 * m))) & ((t // m) % 2 == 1) & ((s // m) % 2 == 0)

    masks = [((t // HG_CHUNK) == (s // HG_CHUNK)) & (s <= t)]
    masks += [level(m) for m in HG_LEVELS] + [t == s]
    tri = np.tril(np.ones((HG_BLOCK, HG_BLOCK), np.float32))
    return np.stack(masks, axis=0).astype(np.float32), np.concatenate([tri, tri], axis=1)


def _attention_rows(r0, n_rows, first_in_seq, sink_ref, q_ref, bias_ref, bands, a_ref,
                    between):
    between = list(between)
    blk = ATT_BLOCK
    ka_s, kb_s, va_s, vb_s = bands
    lane_q = lax.broadcasted_iota(jnp.int32, (blk, LANES), 1)
    lane_kv = lax.broadcasted_iota(jnp.int32, (2 * blk, LANES), 1)
    col = lax.broadcasted_iota(jnp.int32, (blk, 2 * blk), 1)
    lo_q = lane_q < ATT_HEAD_DIM
    q_lo = jnp.where(lo_q, 1.0, 0.0).astype(BF16)
    q_hi = jnp.where(lo_q, 0.0, 1.0).astype(BF16)
    ind_lo = jnp.where(lane_kv < ATT_HEAD_DIM, 1.0, 0.0).astype(BF16)
    ind_hi = jnp.where(lane_kv < ATT_HEAD_DIM, 0.0, 1.0).astype(BF16)
    no_prev = jnp.where(first_in_seq, -jnp.inf, 0.0).astype(F32)

    items = [(q0, g) for q0 in range(r0, r0 + n_rows, blk) for g in range(ATT_HEADS // 2)]

    def sources(g):
        return ("a", "b") if g < 2 else ("b", "a")

    k_band = {q0: {"a": ka_s[q0:q0 + 2 * blk, :], "b": kb_s[q0:q0 + 2 * blk, :]}
              for q0, _ in items}
    scores = []
    for q0, g in items:
        q2 = q_ref[q0:q0 + blk, g * LANES:(g + 1) * LANES]
        scores.append([_dot_nt(q2 * sel, k_band[q0][src])
                       for sel, src in zip((q_lo, q_hi), sources(g))])

    v_stack = {}
    for q0 in sorted(k_band):
        v_a = va_s[q0:q0 + 2 * blk, :]
        v_b = vb_s[q0:q0 + 2 * blk, :]
        for even, odd, key in ((v_a, v_b, ("a", "b")), (v_b, v_a, ("b", "a"))):
            v_stack[q0, key] = jnp.concatenate(
                [jnp.concatenate([even * ind_lo, ind_lo], axis=1),
                 jnp.concatenate([odd * ind_hi, ind_hi], axis=1)], axis=0)

    probs, sink_terms = [], []
    for (q0, g), pair in zip(items, scores):
        p_pair, t_pair = [], []
        for half, s in enumerate(pair):
            head = 2 * g + half
            s = s + bias_ref[head]
            if q0 == 0:
                s = s + jnp.where(col < blk, no_prev, 0.0)
            sink = sink_ref[head]
            m = jnp.maximum(jnp.max(s, axis=-1, keepdims=True), sink)
            p_pair.append(jnp.exp(s - m).astype(BF16))
            t_pair.append(jnp.exp(sink - m))
        probs.append(jnp.concatenate(p_pair, axis=1))
        sink_terms.append(jnp.where(lo_q, t_pair[0], t_pair[1]))
        if between:
            between.pop(0)()
    for step in between:
        step()

    results = [_dot(p, v_stack[q0, sources(g)]) for (q0, g), p in zip(items, probs)]
    for (q0, g), res, sink_term in zip(items, results, sink_terms):
        den = res[:, LANES:] + sink_term
        a_ref[q0:q0 + blk, g * LANES:(g + 1) * LANES] = (res[:, :LANES] / den).astype(BF16)


def _hgrn_block(r0, q_ref, k_ref, ghi_ref, glo_ref, v_ref, gate_ref, tri2, mask_ref,
                st_ref, r_ref, between):
    between = list(between)

    def run_one():
        if between:
            between.pop(0)()

    N = HG_BLOCK
    H = N // 2
    C = HG_CHUNK
    W = q_ref.shape[1]
    rows = slice(r0, r0 + N)
    heads = [slice(h * HG_DK, (h + 1) * HG_DK) for h in range(HG_HEADS)]
    halves = [slice(0, H), slice(H, N)]
    qq = q_ref[rows, :]
    kk = k_ref[rows, :]
    vv = v_ref[rows, :]

    g_pieces = jnp.concatenate([ghi_ref[rows, :], glo_ref[rows, :]], axis=0)
    b = _dot(tri2, g_pieces)
    b_last = b[N - 1:N, :]
    decay = jnp.exp2(b_last)

    def neg_offset(m):
        parts = []
        for k in range(0, N, 2 * m):
            ref = b[k + m - 1:k + m, :]
            parts += [ref - b[k:k + m, :], b[k + m:k + 2 * m, :] - ref]
        return jnp.concatenate(parts, axis=0)

    def finish(atts, cross, o_state):
        probs = [(top.astype(BF16), jnp.concatenate([cross[h], bot], axis=1).astype(BF16))
                 for h, (top, bot) in enumerate(atts)]
        outs = [jnp.concatenate([_dot(p_top, vv[0:H, sl]), _dot(p_bot, vv[:, sl])], axis=0)
                + o_state[h] for h, (sl, (p_top, p_bot)) in enumerate(zip(heads, probs))]
        for sl, o in zip(heads, outs):
            r = o * _rms_scale(o)
            r_ref[rows, sl] = (r * gate_ref[rows, sl].astype(F32)).astype(BF16)

    q_state = qq * jnp.exp2(b).astype(BF16)
    run_one()
    k_state = kk * jnp.exp2(b_last - b).astype(BF16)
    run_one()

    e128 = jnp.exp2(neg_offset(H)).astype(BF16)
    q128 = qq[H:N, :] * e128[H:N, :]
    k128 = kk[0:H, :] * e128[0:H, :]
    run_one()
    e64 = jnp.exp2(neg_offset(C)).astype(BF16)
    q64 = [qq[r + C:r + 2 * C, :] * e64[r + C:r + 2 * C, :] for r in (0, H)]
    k64 = [kk[r:r + C, :] * e64[r:r + C, :] for r in (0, H)]
    run_one()

    so_mid = neg_offset(C // 2)
    span = -jnp.min(so_mid)
    e_mid = jnp.exp2(so_mid)
    small = e_mid.astype(BF16)
    large = (1.0 / e_mid).astype(BF16)
    q_sc, k_sc = [], []
    for k in range(0, N, C):
        q_sc += [large[k:k + C // 2, :], small[k + C // 2:k + C, :]]
        k_sc += [small[k:k + C // 2, :], large[k + C // 2:k + C, :]]
    q_mid = qq * jnp.concatenate(q_sc, axis=0)
    k_mid = kk * jnp.concatenate(k_sc, axis=0)
    for step in between:
        step()
    in_chunk = mask_ref[0] > 0.5
    pad = jnp.zeros((C, H - C), F32)

    def with_level64(fine, p64):
        return jnp.concatenate(
            [fine[0:C, :], fine[C:H, :] + jnp.concatenate([p64, pad], axis=1)], axis=0)

    mid = [[_dot_nt(q_mid[hv, sl], k_mid[hv, sl]) for hv in halves] for sl in heads]
    lvl64 = [[_dot_nt(q64[i][:, sl], k64[i][:, sl]) for i in range(2)] for sl in heads]
    cross = [_dot_nt(q128[:, sl], k128[:, sl]) for sl in heads]
    o_state = []
    for h, sl in enumerate(heads):
        st = st_ref[h]
        o_state.append(_dot_nt(q_state[:, sl], st.astype(BF16)))
        st_ref[h] = st * decay[:, sl] + _dot_tn(vv[:, sl], k_state[:, sl])
    finish([[with_level64(jnp.where(in_chunk, mid[h][i], 0.0), lvl64[h][i]) for i in range(2)]
            for h in range(HG_HEADS)], cross, o_state)

    @pl.when(span >= HG_SAFE_SPAN)
    def _():
        def offset(m):
            ref = [jnp.broadcast_to(b[k + m - 1:k + m, :], (2 * m, W)) for k in range(0, N, 2 * m)]
            return b - jnp.concatenate(ref, axis=0)

        f = jnp.exp2(ghi_ref[rows, :].astype(F32) + glo_ref[rows, :].astype(F32))
        scales = [e_mid]
        scales += [jnp.exp2(-jnp.abs(offset(m))) for m in HG_LEVELS[1:] if 2 * m >= 8]
        row = lax.broadcasted_iota(jnp.int32, (N, W), 0)
        f_next = pltpu.roll(f, N - 1, 0)
        f_prev = pltpu.roll(f, 1, 0)
        r4 = row & 3
        scales.append(jnp.where(r4 == 0, f_next,
                                jnp.where(r4 == 1, 1.0, jnp.where(r4 == 2, f, f * f_prev))))
        scales.append(jnp.where((row & 1) == 1, f, 1.0))
        pairs = [(qq * e.astype(BF16), kk * e.astype(BF16)) for e in scales] + [(qq, kk)]
        atts = []
        for h, sl in enumerate(heads):
            att = []
            for i, hv in enumerate(halves):
                fine = jnp.zeros((H, H), F32)
                for j, (ql, kl) in enumerate(pairs):
                    fine = fine + _dot_nt(ql[hv, sl], kl[hv, sl]) * mask_ref[1 + j]
                att.append(with_level64(fine, lvl64[h][i]))
            atts.append(att)
        finish(atts, cross, o_state)


def _merge_steps(rows, a_ref, r_ref, sga_ref, sgr_ref, x_ref, wpa_ref, wph_ref, wout_ref, n2,
                 m_ref, x1_ref, h2_ref):
    def gated(cols):
        def run():
            pa = _dot(a_ref[rows, :], wpa_ref[:, cols])
            ph = _dot(r_ref[rows, :], wph_ref[:, cols])
            m_ref[rows, cols] = (sga_ref[rows, cols].astype(F32) * pa
                                 + sgr_ref[rows, cols].astype(F32) * ph).astype(BF16)
        return run

    def projected(cols):
        def run():
            x1_ref[rows, cols] = x_ref[rows, cols] + _dot(m_ref[rows, :], wout_ref[:, cols])
        return run

    def normed():
        x1 = x1_ref[rows, :]
        h2_ref[rows, :] = (x1 * _rms_scale(x1) * n2).astype(BF16)

    tiles = [slice(c, c + MERGE_CW) for c in range(0, D_MODEL, MERGE_CW)]
    return [gated(c) for c in tiles] + [projected(c) for c in tiles] + [normed]


def _mixer_kernel(sink_ref, qa_ref, kac_ref, kbc_ref, vac_ref, vbc_ref,
                  kap_ref, kbp_ref, vap_ref, vbp_ref, bias_ref,
                  hq_ref, hk_ref, ghi_ref, glo_ref, hv_ref, hgate_ref, tri_ref, mask_ref,
                  sga_ref, sgr_ref, x_ref, wpa_ref, wph_ref, wout_ref, n2_ref,
                  x1_ref, h2_ref,
                  ka_s, kb_s, va_s, vb_s, a_s, r_s, m_s, st_ref):
    blk = ATT_BLOCK
    tile = qa_ref.shape[0]
    first_in_seq = pl.program_id(1) == 0

    @pl.when(first_in_seq)
    def _():
        st_ref[...] = jnp.zeros_like(st_ref)

    for s_ref, p_ref, c_ref in ((ka_s, kap_ref, kac_ref), (kb_s, kbp_ref, kbc_ref),
                                (va_s, vap_ref, vac_ref), (vb_s, vbp_ref, vbc_ref)):
        s_ref[0:blk, :] = p_ref[...]
        s_ref[blk:, :] = c_ref[...]

    tri = tri_ref[...]
    n2 = n2_ref[...]
    pending = []
    n_gated = D_MODEL // MERGE_CW
    for r0 in range(0, tile, HG_BLOCK):
        _attention_rows(r0, HG_BLOCK, first_in_seq, sink_ref, qa_ref, bias_ref,
                        (ka_s, kb_s, va_s, vb_s), a_s, pending[:n_gated])
        _hgrn_block(r0, hq_ref, hk_ref, ghi_ref, glo_ref, hv_ref, hgate_ref, tri, mask_ref,
                    st_ref, r_s, pending[n_gated:])
        pending = _merge_steps(slice(r0, r0 + HG_BLOCK), a_s, r_s, sga_ref, sgr_ref, x_ref,
                               wpa_ref, wph_ref, wout_ref, n2, m_s, x1_ref, h2_ref)
    for step in pending:
        step()


def _mixer(sinks, qa, ka, kb, va, vb, bias_tab, hq, hk, ghi, glo, hv, hgate,
           sga, sgr, xf, wpa, wph, wout, n2, B, S):
    T = qa.shape[0]
    tm = TM_MIX
    blk = ATT_BLOCK
    per_seq = S // tm
    ratio = tm // blk
    masks_np, tri_np = _hgrn_masks()
    masks = jnp.asarray(masks_np, F32)
    tri = jnp.asarray(tri_np, BF16)
    cur = lambda n: pl.BlockSpec((tm, n), lambda b, i: (b * per_seq + i, 0))
    prev = pl.BlockSpec(
        (blk, ATT_KV), lambda b, i: (b * (S // blk) + jnp.maximum(i * ratio - 1, 0), 0))
    return pl.pallas_call(
        _mixer_kernel,
        grid=(B, per_seq),
        in_specs=[pl.BlockSpec(memory_space=pltpu.SMEM), cur(ATT_Q),
                  cur(ATT_KV), cur(ATT_KV), cur(ATT_KV), cur(ATT_KV),
                  prev, prev, prev, prev,
                  _const_spec((ATT_HEADS, blk, 2 * blk)),
                  cur(HG_K), cur(HG_K), cur(HG_K), cur(HG_K), cur(HG_V), cur(HG_V),
                  _const_spec(tri.shape), _const_spec(masks.shape),
                  cur(D_MODEL), cur(D_MODEL), cur(D_MODEL),
                  _const_spec((ATT_Q, D_MODEL)), _const_spec((HG_V, D_MODEL)),
                  _const_spec((D_MODEL, D_MODEL)), _const_spec((1, D_MODEL))],
        out_specs=[cur(D_MODEL), cur(D_MODEL)],
        out_shape=[jax.ShapeDtypeStruct((T, D_MODEL), F32),
                   jax.ShapeDtypeStruct((T, D_MODEL), BF16)],
        scratch_shapes=[pltpu.VMEM((tm + blk, ATT_KV), BF16)] * 4
        + [pltpu.VMEM((tm, ATT_Q), BF16), pltpu.VMEM((tm, HG_V), BF16),
           pltpu.VMEM((tm, D_MODEL), BF16), pltpu.VMEM((HG_HEADS, HG_DV, HG_DK), F32)],
        compiler_params=_params(),
        name="mixer",
    )(sinks, qa, ka, kb, va, vb, ka, kb, va, vb, bias_tab,
      hq, hk, ghi, glo, hv, hgate, tri, masks, sga, sgr, xf, wpa, wph, wout, n2)


def _ffn_kernel(h2_ref, x1_ref, wup_ref, cw_ref, cb_ref, wdn_ref, o_ref,
                carry_ref, act_ref, *, tiles_per_seq):
    tm = h2_ref.shape[0]
    halo = carry_ref.shape[0]

    @pl.when(pl.program_id(0) % tiles_per_seq == 0)
    def _():
        carry_ref[...] = jnp.zeros_like(carry_ref)

    h2 = h2_ref[...]
    head_row = lax.broadcasted_iota(jnp.int32, (halo, FFN_CW), 0)

    def shifted(u, prev, n):
        s = pltpu.roll(u, n, 0)
        head = jnp.where(head_row < n, pltpu.roll(prev, n, 0), s[0:halo, :])
        return jnp.concatenate([head, s[halo:, :]], axis=0)

    def conv(off):
        cols = slice(off, off + FFN_CW)
        u = _dot(h2, wup_ref[:, cols])
        prev = carry_ref[:, cols]
        carry_ref[:, cols] = u[tm - halo:tm, :]
        w = cw_ref[:, cols]
        return (w[0:1, :] * shifted(u, prev, 2) + w[1:2, :] * shifted(u, prev, 1)
                + w[2:3, :] * u + cb_ref[:, cols])

    for c in range(D_FF // FFN_CW):
        ug = conv(c * FFN_CW)
        uv = conv(D_FF + c * FFN_CW)
        act_ref[:, c * FFN_CW:(c + 1) * FFN_CW] = (ug * _sigmoid(ug) * uv).astype(BF16)
    o_ref[...] = x1_ref[...] + _dot(act_ref[...], wdn_ref[...])


def _ffn(h2, x1, wup, cw, cb, wdn, S):
    T = h2.shape[0]
    tm = TM_FFN
    halo = 8
    row = pl.BlockSpec((tm, D_MODEL), lambda i: (i, 0))
    return pl.pallas_call(
        functools.partial(_ffn_kernel, tiles_per_seq=S // tm),
        grid=(T // tm,),
        in_specs=[row, row, _const_spec((D_MODEL, 2 * D_FF)),
                  _const_spec((CONV_WIDTH, 2 * D_FF)), _const_spec((1, 2 * D_FF)),
                  _const_spec((D_FF, D_MODEL))],
        out_specs=row,
        out_shape=jax.ShapeDtypeStruct((T, D_MODEL), F32),
        scratch_shapes=[pltpu.VMEM((halo, 2 * D_FF), F32),
                        pltpu.VMEM((tm, D_FF), BF16)],
        compiler_params=_params(),
        name="conv_ffn",
    )(h2, x1, wup, cw, cb, wdn)


def _bias_table(rel_bias):
    blk = ATT_BLOCK
    assert WINDOW == blk
    d = np.arange(WINDOW)
    max_exact = N_BUCKETS // 2
    large = max_exact + (np.log(np.maximum(d, 1) / max_exact) / math.log(MAX_DISTANCE / max_exact)
                         * (N_BUCKETS - max_exact)).astype(np.int32)
    bucket = np.where(d < max_exact, d, np.minimum(large, N_BUCKETS - 1))
    onehot = jnp.asarray(bucket[:, None] == np.arange(N_BUCKETS)[None, :], F32)
    per_dist = jnp.dot(onehot, rel_bias.astype(F32), precision=lax.Precision.HIGHEST)
    heads = per_dist.shape[1]
    period = 3 * blk
    z = jnp.concatenate([jnp.full((heads, 1), -jnp.inf, F32), per_dist[::-1].T,
                         jnp.full((heads, period - WINDOW - 1), -jnp.inf, F32)], axis=1)
    skew = jnp.tile(z, (1, blk))[:, :blk * (period - 1)].reshape(heads, blk, period - 1)
    return skew[:, :, :2 * blk]


def _block_diag_ones(n, blk):
    idx = np.arange(n) // blk
    return jnp.asarray((idx[:, None] == idx[None, :]).astype(np.float32), BF16)


def kernel(x, norm1, w_in, q_norm, k_norm, sinks, rel_bias, hg_lb, hg_norm,
           w_pa, w_ph, w_out, norm2, w_up, conv_w, conv_b, w_down):
    B, S, D = x.shape
    T = B * S
    depth = w_in.shape[0]
    assert D == D_MODEL and T % TM_PROJ == 0 and S % TM_FFN == 0 and S % TM_MIX == 0
    assert TM_MIX % HG_BLOCK == 0 and HG_BLOCK % (2 * ATT_BLOCK) == 0

    p_lb = jax.nn.softmax(hg_lb.astype(F32), axis=0)
    lbs = jnp.cumsum(p_lb, axis=0) - p_lb[0:1]

    bias_tab = _bias_table(rel_bias)
    bdq = _block_diag_ones(ATT_Q, ATT_HEAD_DIM)
    bdk = _block_diag_ones(ATT_KV, ATT_HEAD_DIM)
    scale = ATT_HEAD_DIM ** -0.5

    xf = x.reshape(T, D)
    for l in range(depth):
        qg = (jnp.tile(q_norm[l].astype(F32), ATT_HEADS) * scale).reshape(1, ATT_Q)
        kg = jnp.tile(k_norm[l].astype(F32), ATT_KV_HEADS).reshape(1, ATT_KV)
        (qa, ka, kb, va, vb, hq, hk, ghi, glo, hv, hgate, sga, sgr) = _inproj(
            xf, norm1[l].reshape(1, D), w_in[l].astype(BF16), qg, kg, bdq, bdk,
            lbs[l].reshape(1, HG_K),
            jnp.tile(hg_norm[l].astype(F32), HG_HEADS).reshape(1, HG_V))
        x1, h2 = _mixer(sinks[l].astype(F32), qa, ka, kb, va, vb, bias_tab,
                        hq, hk, ghi, glo, hv, hgate,
                        sga, sgr, xf, w_pa[l].astype(BF16), w_ph[l].astype(BF16),
                        w_out[l].astype(BF16), norm2[l].reshape(1, D), B, S)
        xf = _ffn(h2, x1, w_up[l].astype(BF16), conv_w[l].astype(F32),
                  conv_b[l].reshape(1, 2 * D_FF).astype(F32), w_down[l].astype(BF16), S)
    return xf.reshape(B, S, D)
```

```python
import functools
import math

import numpy as np
import jax
import jax.numpy as jnp
from jax import lax
from jax.experimental import pallas as pl
from jax.experimental.pallas import tpu as pltpu

F32 = jnp.float32
BF16 = jnp.bfloat16

D_MODEL = 1024
ATT_HEADS = 8
ATT_KV_HEADS = 2
ATT_HEAD_DIM = 64
WINDOW = 128
ATT_BLOCK = 128
N_BUCKETS = 32
MAX_DISTANCE = 128
HG_HEADS = 4
HG_DK = 128
HG_DV = 128
HG_CHUNK = 64
D_FF = 2816
CONV_WIDTH = 3
EPS = 1e-6

ATT_Q = ATT_HEADS * ATT_HEAD_DIM
ATT_KV = ATT_KV_HEADS * ATT_HEAD_DIM
HG_K = HG_HEADS * HG_DK
HG_V = HG_HEADS * HG_DV
IN_SPLITS = (ATT_Q, ATT_KV, ATT_KV, HG_K, HG_K, HG_V, HG_V, D_MODEL, D_MODEL)
IN_OFFS = tuple(int(o) for o in np.cumsum((0,) + IN_SPLITS))
D_IN = IN_OFFS[-1]

LANES = 128
VMEM_LIMIT_BYTES = 56 * 1024 * 1024

TM_PROJ = 512
NORM_GROUP = 256
TM_FFN = 1024
FFN_CW = 256
TM_MIX = 512
MERGE_CW = 256
HG_BLOCK = 256
HG_LEVELS = (32, 16, 8, 4, 2, 1)
HG_SAFE_SPAN = 80.0


def _dot(a, b):
    return jnp.dot(a, b, preferred_element_type=F32)


def _dot_nt(a, b):
    return lax.dot_general(a, b, (((1,), (1,)), ((), ())), preferred_element_type=F32)


def _dot_tn(a, b):
    return lax.dot_general(a, b, (((0,), (0,)), ((), ())), preferred_element_type=F32)


def _sigmoid(x):
    return 1.0 / (1.0 + jnp.exp(-x))


def _rms_scale(x):
    return lax.rsqrt(jnp.mean(x * x, axis=-1, keepdims=True) + EPS)


def _const_spec(shape):
    nd = len(shape)
    return pl.BlockSpec(shape, lambda *_: (0,) * nd, pipeline_mode=pl.Buffered(1))


def _layer_spec(stacked, layer):
    _, rows, cols = stacked.shape
    return pl.BlockSpec((None, rows, cols), lambda *_: (layer, 0, 0),
                        pipeline_mode=pl.Buffered(1))


def _params():
    return pltpu.CompilerParams(vmem_limit_bytes=VMEM_LIMIT_BYTES)


def _inproj_kernel(x_ref, n1_ref, w_ref, qg_ref, kg_ref, bd_ref, lb_ref, hn_ref,
                   qa_ref, ka_ref, kb_ref, va_ref, vb_ref,
                   hq_ref, hk_ref, ghi_ref, glo_ref, hv_ref, hgate_ref, sga_ref, sgr_ref):
    x = x_ref[...]
    h = (x * _rms_scale(x) * n1_ref[...]).astype(BF16)

    def proj(seg):
        return _dot(h, w_ref[:, IN_OFFS[seg]:IN_OFFS[seg + 1]])

    bd = bd_ref[...]
    q = proj(0)
    q_sq = (q * q).astype(BF16)
    q_ms = jnp.concatenate([_dot(q_sq[:, c:c + NORM_GROUP], bd)
                            for c in range(0, ATT_Q, NORM_GROUP)], axis=1) * (1.0 / ATT_HEAD_DIM)
    qa_ref[...] = (q * lax.rsqrt(q_ms + EPS) * qg_ref[...]).astype(BF16)

    k = proj(1)
    k_ms = _dot((k * k).astype(BF16), bd[0:ATT_KV, 0:ATT_KV]) * (1.0 / ATT_HEAD_DIM)
    kn = k * lax.rsqrt(k_ms + EPS) * kg_ref[...]
    ka_ref[...] = kn.astype(BF16)
    kb_ref[...] = pltpu.roll(kn, ATT_HEAD_DIM, 1).astype(BF16)

    v = proj(2)
    va_ref[...] = v.astype(BF16)
    vb_ref[...] = pltpu.roll(v, ATT_HEAD_DIM, 1).astype(BF16)

    qr = proj(3)
    hq_ref[...] = (qr * _sigmoid(qr)).astype(BF16)
    lb = lb_ref[...]
    f = lb + (1.0 - lb) * _sigmoid(proj(4))
    hk_ref[...] = (1.0 - f).astype(BF16)
    g = jnp.log2(f)
    g_hi = g.astype(BF16)
    ghi_ref[...] = g_hi
    glo_ref[...] = (g - g_hi.astype(F32)).astype(BF16)
    hv_ref[...] = proj(5).astype(BF16)
    gr = proj(6)
    hgate_ref[...] = (gr * _sigmoid(gr) * hn_ref[...]).astype(BF16)
    sga_ref[...] = _sigmoid(proj(7)).astype(BF16)
    sgr_ref[...] = _sigmoid(proj(8)).astype(BF16)


def _inproj(xf, n1, w_in, layer, qg, kg, bd, lb, hn):
    T = xf.shape[0]
    tm = TM_PROJ
    row = lambda n: pl.BlockSpec((tm, n), lambda i: (i, 0))
    out_widths = [ATT_Q, ATT_KV, ATT_KV, ATT_KV, ATT_KV,
                  HG_K, HG_K, HG_K, HG_K, HG_V, HG_V, D_MODEL, D_MODEL]
    return pl.pallas_call(
        _inproj_kernel,
        grid=(T // tm,),
        in_specs=[row(D_MODEL), _const_spec((1, D_MODEL)), _layer_spec(w_in, layer),
                  _const_spec((1, ATT_Q)), _const_spec((1, ATT_KV)),
                  _const_spec((NORM_GROUP, NORM_GROUP)),
                  _const_spec((1, HG_K)), _const_spec((1, HG_V))],
        out_specs=[row(n) for n in out_widths],
        out_shape=[jax.ShapeDtypeStruct((T, n), BF16) for n in out_widths],
        compiler_params=_params(),
        name="inproj",
    )(xf, n1, w_in, qg, kg, bd, lb, hn)


def _hgrn_masks():
    n = HG_BLOCK // 2
    t = np.arange(n)[:, None]
    s = np.arange(n)[None, :]

    def level(m):
        return ((t // (2 * m)) == (s // (2 * m))) & ((t // m) % 2 == 1) & ((s // m) % 2 == 0)

    masks = [((t // HG_CHUNK) == (s // HG_CHUNK)) & (s <= t)]
    masks += [level(m) for m in HG_LEVELS] + [t == s]
    tri = np.tril(np.ones((HG_BLOCK, HG_BLOCK), np.float32))
    return np.stack(masks, axis=0).astype(np.float32), np.concatenate([tri, tri], axis=1)


def _attention_rows(r0, n_rows, first_in_seq, sink_ref, q_ref, bias_ref, bands, a_ref,
                    between):
    between = list(between)
    blk = ATT_BLOCK
    ka_s, kb_s, va_s, vb_s = bands
    lane_q = lax.broadcasted_iota(jnp.int32, (blk, LANES), 1)
    lane_kv = lax.broadcasted_iota(jnp.int32, (2 * blk, LANES), 1)
    col = lax.broadcasted_iota(jnp.int32, (blk, 2 * blk), 1)
    lo_q = lane_q < ATT_HEAD_DIM
    q_lo = jnp.where(lo_q, 1.0, 0.0).astype(BF16)
    q_hi = jnp.where(lo_q, 0.0, 1.0).astype(BF16)
    ind_lo = jnp.where(lane_kv < ATT_HEAD_DIM, 1.0, 0.0).astype(BF16)
    ind_hi = jnp.where(lane_kv < ATT_HEAD_DIM, 0.0, 1.0).astype(BF16)
    no_prev = jnp.where(first_in_seq, -jnp.inf, 0.0).astype(F32)

    items = [(q0, g) for q0 in range(r0, r0 + n_rows, blk) for g in range(ATT_HEADS // 2)]

    def sources(g):
        return ("a", "b") if g < 2 else ("b", "a")

    k_band = {q0: {"a": ka_s[q0:q0 + 2 * blk, :], "b": kb_s[q0:q0 + 2 * blk, :]}
              for q0, _ in items}
    scores = []
    for q0, g in items:
        q2 = q_ref[q0:q0 + blk, g * LANES:(g + 1) * LANES]
        scores.append([_dot_nt(q2 * sel, k_band[q0][src])
                       for sel, src in zip((q_lo, q_hi), sources(g))])

    v_stack = {}
    for q0 in sorted(k_band):
        v_a = va_s[q0:q0 + 2 * blk, :]
        v_b = vb_s[q0:q0 + 2 * blk, :]
        for even, odd, key in ((v_a, v_b, ("a", "b")), (v_b, v_a, ("b", "a"))):
            v_stack[q0, key] = jnp.concatenate(
                [jnp.concatenate([even * ind_lo, ind_lo], axis=1),
                 jnp.concatenate([odd * ind_hi, ind_hi], axis=1)], axis=0)

    probs, sink_terms = [], []
    for (q0, g), pair in zip(items, scores):
        p_pair, t_pair = [], []
        for half, s in enumerate(pair):
            head = 2 * g + half
            s = s + bias_ref[head]
            if q0 == 0:
                s = s + jnp.where(col < blk, no_prev, 0.0)
            sink = sink_ref[head]
            m = jnp.maximum(jnp.max(s, axis=-1, keepdims=True), sink)
            p_pair.append(jnp.exp2(s - m).astype(BF16))
            t_pair.append(jnp.exp2(sink - m))
        probs.append(jnp.concatenate(p_pair, axis=1))
        sink_terms.append(jnp.where(lo_q, t_pair[0], t_pair[1]))
        if between:
            between.pop(0)()
    for step in between:
        step()

    results = [_dot(p, v_stack[q0, sources(g)]) for (q0, g), p in zip(items, probs)]
    for (q0, g), res, sink_term in zip(items, results, sink_terms):
        den = res[:, LANES:] + sink_term
        a_ref[q0:q0 + blk, g * LANES:(g + 1) * LANES] = (res[:, :LANES] / den).astype(BF16)


def _hgrn_block(r0, q_ref, k_ref, ghi_ref, glo_ref, v_ref, gate_ref, tri2, mask_ref,
                st_ref, r_ref, between):
    between = list(between)

    def run_one():
        if between:
            between.pop(0)()

    N = HG_BLOCK
    H = N // 2
    C = HG_CHUNK
    W = q_ref.shape[1]
    rows = slice(r0, r0 + N)
    heads = [slice(h * HG_DK, (h + 1) * HG_DK) for h in range(HG_HEADS)]
    halves = [slice(0, H), slice(H, N)]
    qq = q_ref[rows, :]
    kk = k_ref[rows, :]
    vv = v_ref[rows, :]

    g_pieces = jnp.concatenate([ghi_ref[rows, :], glo_ref[rows, :]], axis=0)
    b = _dot(tri2, g_pieces)
    b_last = b[N - 1:N, :]
    decay = jnp.exp2(b_last)

    def neg_offset(m):
        parts = []
        for k in range(0, N, 2 * m):
            ref = b[k + m - 1:k + m, :]
            parts += [ref - b[k:k + m, :], b[k + m:k + 2 * m, :] - ref]
        return jnp.concatenate(parts, axis=0)

    def finish(atts, cross, o_state):
        probs = [(top.astype(BF16), jnp.concatenate([cross[h], bot], axis=1).astype(BF16))
                 for h, (top, bot) in enumerate(atts)]
        outs = [jnp.concatenate([_dot(p_top, vv[0:H, sl]), _dot(p_bot, vv[:, sl])], axis=0)
                + o_state[h] for h, (sl, (p_top, p_bot)) in enumerate(zip(heads, probs))]
        for sl, o in zip(heads, outs):
            r = o * _rms_scale(o)
            r_ref[rows, sl] = (r * gate_ref[rows, sl].astype(F32)).astype(BF16)

    q_state = qq * jnp.exp2(b).astype(BF16)
    run_one()
    k_state = kk * jnp.exp2(b_last - b).astype(BF16)
    run_one()

    e128 = jnp.exp2(neg_offset(H)).astype(BF16)
    q128 = qq[H:N, :] * e128[H:N, :]
    k128 = kk[0:H, :] * e128[0:H, :]
    run_one()
    e64 = jnp.exp2(neg_offset(C)).astype(BF16)
    q64 = [qq[r + C:r + 2 * C, :] * e64[r + C:r + 2 * C, :] for r in (0, H)]
    k64 = [kk[r:r + C, :] * e64[r:r + C, :] for r in (0, H)]
    run_one()

    so_mid = neg_offset(C // 2)
    span = -jnp.min(so_mid)
    e_mid = jnp.exp2(so_mid)
    small = e_mid.astype(BF16)
    large = (1.0 / e_mid).astype(BF16)
    q_sc, k_sc = [], []
    for k in range(0, N, C):
        q_sc += [large[k:k + C // 2, :], small[k + C // 2:k + C, :]]
        k_sc += [small[k:k + C // 2, :], large[k + C // 2:k + C, :]]
    q_mid = qq * jnp.concatenate(q_sc, axis=0)
    k_mid = kk * jnp.concatenate(k_sc, axis=0)
    for step in between:
        step()
    in_chunk = mask_ref[0] > 0.5
    pad = jnp.zeros((C, H - C), F32)

    def with_level64(fine, p64):
        return jnp.concatenate(
            [fine[0:C, :], fine[C:H, :] + jnp.concatenate([p64, pad], axis=1)], axis=0)

    mid = [[_dot_nt(q_mid[hv, sl], k_mid[hv, sl]) for hv in halves] for sl in heads]
    lvl64 = [[_dot_nt(q64[i][:, sl], k64[i][:, sl]) for i in range(2)] for sl in heads]
    cross = [_dot_nt(q128[:, sl], k128[:, sl]) for sl in heads]
    o_state = []
    for h, sl in enumerate(heads):
        st = st_ref[h]
        o_state.append(_dot_nt(q_state[:, sl], st.astype(BF16)))
        st_ref[h] = st * decay[:, sl] + _dot_tn(vv[:, sl], k_state[:, sl])
    finish([[with_level64(jnp.where(in_chunk, mid[h][i], 0.0), lvl64[h][i]) for i in range(2)]
            for h in range(HG_HEADS)], cross, o_state)

    @pl.when(span >= HG_SAFE_SPAN)
    def _():
        def offset(m):
            ref = [jnp.broadcast_to(b[k + m - 1:k + m, :], (2 * m, W)) for k in range(0, N, 2 * m)]
            return b - jnp.concatenate(ref, axis=0)

        f = jnp.exp2(ghi_ref[rows, :].astype(F32) + glo_ref[rows, :].astype(F32))
        scales = [e_mid]
        scales += [jnp.exp2(-jnp.abs(offset(m))) for m in HG_LEVELS[1:] if 2 * m >= 8]
        row = lax.broadcasted_iota(jnp.int32, (N, W), 0)
        f_next = pltpu.roll(f, N - 1, 0)
        f_prev = pltpu.roll(f, 1, 0)
        r4 = row & 3
        scales.append(jnp.where(r4 == 0, f_next,
                                jnp.where(r4 == 1, 1.0, jnp.where(r4 == 2, f, f * f_prev))))
        scales.append(jnp.where((row & 1) == 1, f, 1.0))
        pairs = [(qq * e.astype(BF16), kk * e.astype(BF16)) for e in scales] + [(qq, kk)]
        atts = []
        for h, sl in enumerate(heads):
            att = []
            for i, hv in enumerate(halves):
                fine = jnp.zeros((H, H), F32)
                for j, (ql, kl) in enumerate(pairs):
                    fine = fine + _dot_nt(ql[hv, sl], kl[hv, sl]) * mask_ref[1 + j]
                att.append(with_level64(fine, lvl64[h][i]))
            atts.append(att)
        finish(atts, cross, o_state)


def _merge_steps(rows, a_ref, r_ref, sga_ref, sgr_ref, x_ref, wpa_ref, wph_ref, wout_ref, n2,
                 m_ref, x1_ref, h2_ref):
    def gated(cols):
        def run():
            pa = _dot(a_ref[rows, :], wpa_ref[:, cols])
            ph = _dot(r_ref[rows, :], wph_ref[:, cols])
            m_ref[rows, cols] = (sga_ref[rows, cols].astype(F32) * pa
                                 + sgr_ref[rows, cols].astype(F32) * ph).astype(BF16)
        return run

    def projected(cols):
        def run():
            x1_ref[rows, cols] = x_ref[rows, cols] + _dot(m_ref[rows, :], wout_ref[:, cols])
        return run

    def normed():
        x1 = x1_ref[rows, :]
        h2_ref[rows, :] = (x1 * _rms_scale(x1) * n2).astype(BF16)

    tiles = [slice(c, c + MERGE_CW) for c in range(0, D_MODEL, MERGE_CW)]
    return [gated(c) for c in tiles] + [projected(c) for c in tiles] + [normed]


def _mixer_kernel(sink_ref, qa_ref, kac_ref, kbc_ref, vac_ref, vbc_ref,
                  kap_ref, kbp_ref, vap_ref, vbp_ref, bias_ref,
                  hq_ref, hk_ref, ghi_ref, glo_ref, hv_ref, hgate_ref, tri_ref, mask_ref,
                  sga_ref, sgr_ref, x_ref, wpa_ref, wph_ref, wout_ref, n2_ref,
                  x1_ref, h2_ref,
                  ka_s, kb_s, va_s, vb_s, a_s, r_s, m_s, st_ref):
    blk = ATT_BLOCK
    tile = qa_ref.shape[0]
    first_in_seq = pl.program_id(1) == 0

    @pl.when(first_in_seq)
    def _():
        st_ref[...] = jnp.zeros_like(st_ref)

    for s_ref, p_ref, c_ref in ((ka_s, kap_ref, kac_ref), (kb_s, kbp_ref, kbc_ref),
                                (va_s, vap_ref, vac_ref), (vb_s, vbp_ref, vbc_ref)):
        s_ref[0:blk, :] = p_ref[...]
        s_ref[blk:, :] = c_ref[...]

    tri = tri_ref[...]
    n2 = n2_ref[...]
    pending = []
    n_gated = D_MODEL // MERGE_CW
    for r0 in range(0, tile, HG_BLOCK):
        _attention_rows(r0, HG_BLOCK, first_in_seq, sink_ref, qa_ref, bias_ref,
                        (ka_s, kb_s, va_s, vb_s), a_s, pending[:n_gated])
        _hgrn_block(r0, hq_ref, hk_ref, ghi_ref, glo_ref, hv_ref, hgate_ref, tri, mask_ref,
                    st_ref, r_s, pending[n_gated:])
        pending = _merge_steps(slice(r0, r0 + HG_BLOCK), a_s, r_s, sga_ref, sgr_ref, x_ref,
                               wpa_ref, wph_ref, wout_ref, n2, m_s, x1_ref, h2_ref)
    for step in pending:
        step()


def _mixer(sinks, qa, ka, kb, va, vb, bias_tab, hq, hk, ghi, glo, hv, hgate,
           sga, sgr, xf, wpa, wph, wout, layer, n2, B, S):
    T = qa.shape[0]
    tm = TM_MIX
    blk = ATT_BLOCK
    per_seq = S // tm
    ratio = tm // blk
    masks_np, tri_np = _hgrn_masks()
    masks = jnp.asarray(masks_np, F32)
    tri = jnp.asarray(tri_np, BF16)
    cur = lambda n: pl.BlockSpec((tm, n), lambda b, i: (b * per_seq + i, 0))
    prev = pl.BlockSpec(
        (blk, ATT_KV), lambda b, i: (b * (S // blk) + jnp.maximum(i * ratio - 1, 0), 0))
    return pl.pallas_call(
        _mixer_kernel,
        grid=(B, per_seq),
        in_specs=[pl.BlockSpec(memory_space=pltpu.SMEM), cur(ATT_Q),
                  cur(ATT_KV), cur(ATT_KV), cur(ATT_KV), cur(ATT_KV),
                  prev, prev, prev, prev,
                  _const_spec((ATT_HEADS, blk, 2 * blk)),
                  cur(HG_K), cur(HG_K), cur(HG_K), cur(HG_K), cur(HG_V), cur(HG_V),
                  _const_spec(tri.shape), _const_spec(masks.shape),
                  cur(D_MODEL), cur(D_MODEL), cur(D_MODEL),
                  _layer_spec(wpa, layer), _layer_spec(wph, layer), _layer_spec(wout, layer),
                  _const_spec((1, D_MODEL))],
        out_specs=[cur(D_MODEL), cur(D_MODEL)],
        out_shape=[jax.ShapeDtypeStruct((T, D_MODEL), F32),
                   jax.ShapeDtypeStruct((T, D_MODEL), BF16)],
        scratch_shapes=[pltpu.VMEM((tm + blk, ATT_KV), BF16)] * 4
        + [pltpu.VMEM((tm, ATT_Q), BF16), pltpu.VMEM((tm, HG_V), BF16),
           pltpu.VMEM((tm, D_MODEL), BF16), pltpu.VMEM((HG_HEADS, HG_DV, HG_DK), F32)],
        compiler_params=_params(),
        name="mixer",
    )(sinks, qa, ka, kb, va, vb, ka, kb, va, vb, bias_tab,
      hq, hk, ghi, glo, hv, hgate, tri, masks, sga, sgr, xf, wpa, wph, wout, n2)


def _ffn_kernel(h2_ref, x1_ref, wup_ref, cw_ref, cb_ref, wdn_ref, o_ref,
                carry_ref, act_ref, *, tiles_per_seq):
    tm = h2_ref.shape[0]
    halo = carry_ref.shape[0]

    @pl.when(pl.program_id(0) % tiles_per_seq == 0)
    def _():
        carry_ref[...] = jnp.zeros_like(carry_ref)

    h2 = h2_ref[...]
    head_row = lax.broadcasted_iota(jnp.int32, (halo, FFN_CW), 0)

    def shifted(u, prev, n):
        s = pltpu.roll(u, n, 0)
        head = jnp.where(head_row < n, pltpu.roll(prev, n, 0), s[0:halo, :])
        return jnp.concatenate([head, s[halo:, :]], axis=0)

    def conv(off):
        cols = slice(off, off + FFN_CW)
        u = _dot(h2, wup_ref[:, cols])
        prev = carry_ref[:, cols]
        carry_ref[:, cols] = u[tm - halo:tm, :]
        w = cw_ref[:, cols]
        return (w[0:1, :] * shifted(u, prev, 2) + w[1:2, :] * shifted(u, prev, 1)
                + w[2:3, :] * u + cb_ref[:, cols])

    for c in range(D_FF // FFN_CW):
        ug = conv(c * FFN_CW)
        uv = conv(D_FF + c * FFN_CW)
        act_ref[:, c * FFN_CW:(c + 1) * FFN_CW] = (ug * _sigmoid(ug) * uv).astype(BF16)
    o_ref[...] = x1_ref[...] + _dot(act_ref[...], wdn_ref[...])


def _ffn(h2, x1, wup, cw, cb, wdn, layer, S):
    T = h2.shape[0]
    tm = TM_FFN
    halo = 8
    row = pl.BlockSpec((tm, D_MODEL), lambda i: (i, 0))
    return pl.pallas_call(
        functools.partial(_ffn_kernel, tiles_per_seq=S // tm),
        grid=(T // tm,),
        in_specs=[row, row, _layer_spec(wup, layer),
                  _const_spec((CONV_WIDTH, 2 * D_FF)), _const_spec((1, 2 * D_FF)),
                  _layer_spec(wdn, layer)],
        out_specs=row,
        out_shape=jax.ShapeDtypeStruct((T, D_MODEL), F32),
        scratch_shapes=[pltpu.VMEM((halo, 2 * D_FF), F32),
                        pltpu.VMEM((tm, D_FF), BF16)],
        compiler_params=_params(),
        name="conv_ffn",
    )(h2, x1, wup, cw, cb, wdn)


def _bias_table(rel_bias):
    blk = ATT_BLOCK
    assert WINDOW == blk
    d = np.arange(WINDOW)
    max_exact = N_BUCKETS // 2
    large = max_exact + (np.log(np.maximum(d, 1) / max_exact) / math.log(MAX_DISTANCE / max_exact)
                         * (N_BUCKETS - max_exact)).astype(np.int32)
    bucket = np.where(d < max_exact, d, np.minimum(large, N_BUCKETS - 1))
    onehot = jnp.asarray(bucket[:, None] == np.arange(N_BUCKETS)[None, :], F32)
    per_dist = jnp.dot(onehot, rel_bias.astype(F32), precision=lax.Precision.HIGHEST)
    heads = per_dist.shape[1]
    period = 3 * blk
    z = jnp.concatenate([jnp.full((heads, 1), -jnp.inf, F32), per_dist[::-1].T,
                         jnp.full((heads, period - WINDOW - 1), -jnp.inf, F32)], axis=1)
    skew = jnp.tile(z, (1, blk))[:, :blk * (period - 1)].reshape(heads, blk, period - 1)
    return skew[:, :, :2 * blk]


def _block_diag_ones(n, blk):
    idx = np.arange(n) // blk
    return jnp.asarray((idx[:, None] == idx[None, :]).astype(np.float32), BF16)


def kernel(x, norm1, w_in, q_norm, k_norm, sinks, rel_bias, hg_lb, hg_norm,
           w_pa, w_ph, w_out, norm2, w_up, conv_w, conv_b, w_down):
    B, S, D = x.shape
    T = B * S
    depth = w_in.shape[0]
    assert D == D_MODEL and T % TM_PROJ == 0 and S % TM_FFN == 0 and S % TM_MIX == 0
    assert TM_MIX % HG_BLOCK == 0 and HG_BLOCK % (2 * ATT_BLOCK) == 0

    p_lb = jax.nn.softmax(hg_lb.astype(F32), axis=0)
    lbs = jnp.cumsum(p_lb, axis=0) - p_lb[0:1]

    log2e = math.log2(math.e)
    bias_tab = _bias_table(rel_bias) * log2e
    bd = _block_diag_ones(NORM_GROUP, ATT_HEAD_DIM)
    scale = ATT_HEAD_DIM ** -0.5 * log2e

    w_in, w_pa, w_ph, w_out, w_up, w_down = (
        w.astype(BF16) for w in (w_in, w_pa, w_ph, w_out, w_up, w_down))

    xf = x.reshape(T, D)
    for l in range(depth):
        qg = (jnp.tile(q_norm[l].astype(F32), ATT_HEADS) * scale).reshape(1, ATT_Q)
        kg = jnp.tile(k_norm[l].astype(F32), ATT_KV_HEADS).reshape(1, ATT_KV)
        (qa, ka, kb, va, vb, hq, hk, ghi, glo, hv, hgate, sga, sgr) = _inproj(
            xf, norm1[l].reshape(1, D), w_in, l, qg, kg, bd,
            lbs[l].reshape(1, HG_K),
            jnp.tile(hg_norm[l].astype(F32), HG_HEADS).reshape(1, HG_V))
        x1, h2 = _mixer(sinks[l].astype(F32) * log2e, qa, ka, kb, va, vb, bias_tab,
                        hq, hk, ghi, glo, hv, hgate,
                        sga, sgr, xf, w_pa, w_ph, w_out, l, norm2[l].reshape(1, D), B, S)
        xf = _ffn(h2, x1, w_up, conv_w[l].astype(F32),
                  conv_b[l].reshape(1, 2 * D_FF).astype(F32), w_down, l, S)
    return xf.reshape(B, S, D)
```

```python
import functools
import math

import numpy as np
import jax
import jax.numpy as jnp
from jax import lax
from jax.experimental import pallas as pl
from jax.experimental.pallas import tpu as pltpu

F32 = jnp.float32
BF16 = jnp.bfloat16

D_MODEL = 1024
ATT_HEADS = 8
ATT_KV_HEADS = 2
ATT_HEAD_DIM = 64
WINDOW = 128
ATT_BLOCK = 128
N_BUCKETS = 32
MAX_DISTANCE = 128
HG_HEADS = 4
HG_DK = 128
HG_DV = 128
HG_CHUNK = 64
D_FF = 2816
CONV_WIDTH = 3
EPS = 1e-6

ATT_Q = ATT_HEADS * ATT_HEAD_DIM
ATT_KV = ATT_KV_HEADS * ATT_HEAD_DIM
HG_K = HG_HEADS * HG_DK
HG_V = HG_HEADS * HG_DV
IN_SPLITS = (ATT_Q, ATT_KV, ATT_KV, HG_K, HG_K, HG_V, HG_V, D_MODEL, D_MODEL)
IN_OFFS = tuple(int(o) for o in np.cumsum((0,) + IN_SPLITS))
D_IN = IN_OFFS[-1]

LANES = 128
VMEM_LIMIT_BYTES = 56 * 1024 * 1024

PROJ_CW = 256
TM_FFN = 1024
FFN_CW = 256
TM_MIX = 512
MERGE_CW = 256
HG_BLOCK = 256
HG_LEVELS = (32, 16, 8, 4, 2, 1)
HG_SAFE_SPAN = 80.0


def _dot(a, b):
    return jnp.dot(a, b, preferred_element_type=F32)


def _dot_nt(a, b):
    return lax.dot_general(a, b, (((1,), (1,)), ((), ())), preferred_element_type=F32)


def _dot_tn(a, b):
    return lax.dot_general(a, b, (((0,), (0,)), ((), ())), preferred_element_type=F32)


def _sigmoid(x):
    return 1.0 / (1.0 + jnp.exp(-x))


def _rms_scale(x):
    return lax.rsqrt(jnp.mean(x * x, axis=-1, keepdims=True) + EPS)


def _const_spec(shape):
    nd = len(shape)
    return pl.BlockSpec(shape, lambda *_: (0,) * nd, pipeline_mode=pl.Buffered(1))


def _layer_spec(stacked, layer):
    _, rows, cols = stacked.shape
    return pl.BlockSpec((None, rows, cols), lambda *_: (layer, 0, 0),
                        pipeline_mode=pl.Buffered(1))


def _params():
    return pltpu.CompilerParams(vmem_limit_bytes=VMEM_LIMIT_BYTES)


def _inproj_steps(rows, x_ref, n1_ref, w_ref, qg_ref, kg_ref, bd_ref, lb_ref, hn_ref, h_ref, dst):
    band_rows = slice(rows.start + ATT_BLOCK, rows.stop + ATT_BLOCK)

    def normalize():
        x = x_ref[rows, :]
        h_ref[rows, :] = (x * _rms_scale(x) * n1_ref[...]).astype(BF16)

    def head_norm(y, gain, bd):
        ms = _dot((y * y).astype(BF16), bd) * (1.0 / ATT_HEAD_DIM)
        return y * lax.rsqrt(ms + EPS) * gain

    def attn_q(y, cols):
        dst["qa"][rows, cols] = head_norm(y, qg_ref[:, cols], bd_ref[...]).astype(BF16)

    def attn_k(y, cols):
        kn = head_norm(y, kg_ref[...], bd_ref[0:ATT_KV, 0:ATT_KV])
        dst["ka"][band_rows, :] = kn.astype(BF16)
        dst["kb"][band_rows, :] = pltpu.roll(kn, ATT_HEAD_DIM, 1).astype(BF16)

    def attn_v(y, cols):
        dst["va"][band_rows, :] = y.astype(BF16)
        dst["vb"][band_rows, :] = pltpu.roll(y, ATT_HEAD_DIM, 1).astype(BF16)

    def silu_to(name, gain_ref=None):
        def run(y, cols):
            out = y * _sigmoid(y)
            if gain_ref is not None:
                out = out * gain_ref[:, cols]
            dst[name][rows, cols] = out.astype(BF16)
        return run

    def forget(y, cols):
        lb = lb_ref[:, cols]
        f = lb + (1.0 - lb) * _sigmoid(y)
        dst["hk"][rows, cols] = (1.0 - f).astype(BF16)
        g = jnp.log2(f)
        g_hi = g.astype(BF16)
        dst["ghi"][rows, cols] = g_hi
        dst["glo"][rows, cols] = (g - g_hi.astype(F32)).astype(BF16)

    def cast_to(name):
        def run(y, cols):
            dst[name][rows, cols] = y.astype(BF16)
        return run

    def sigmoid_to(name):
        def run(y, cols):
            dst[name][rows, cols] = _sigmoid(y).astype(BF16)
        return run

    epilogues = [attn_q, attn_k, attn_v, silu_to("hq"), forget, cast_to("hv"),
                 silu_to("hgate", hn_ref), sigmoid_to("sga"), sigmoid_to("sgr")]

    def column_step(seg, c0, width):
        def run():
            off = IN_OFFS[seg] + c0
            y = _dot(h_ref[rows, :], w_ref[:, off:off + width])
            epilogues[seg](y, slice(c0, c0 + width))
        return run

    steps = [normalize]
    for seg, total in enumerate(IN_SPLITS):
        width = min(total, PROJ_CW)
        steps += [column_step(seg, c0, width) for c0 in range(0, total, width)]
    return steps


def _hgrn_masks():
    n = HG_BLOCK // 2
    t = np.arange(n)[:, None]
    s = np.arange(n)[None, :]

    def level(m):
        return ((t // (2 * m)) == (s // (2 * m))) & ((t // m) % 2 == 1) & ((s // m) % 2 == 0)

    masks = [((t // HG_CHUNK) == (s // HG_CHUNK)) & (s <= t)]
    masks += [level(m) for m in HG_LEVELS] + [t == s]
    tri = np.tril(np.ones((HG_BLOCK, HG_BLOCK), np.float32))
    return np.stack(masks, axis=0).astype(np.float32), np.concatenate([tri, tri], axis=1)


class _Feeder:
    def __init__(self, steps, slots):
        self.steps = list(steps)
        self.per_slot = -(-len(self.steps) // slots)

    def feed(self):
        for _ in range(self.per_slot):
            if self.steps:
                self.steps.pop(0)()

    def drain(self):
        while self.steps:
            self.steps.pop(0)()


def _attention_rows(r0, n_rows, first_in_seq, sink_ref, q_ref, bias_ref, bands, a_ref,
                    feeder):
    blk = ATT_BLOCK
    ka_s, kb_s, va_s, vb_s = bands
    lane_q = lax.broadcasted_iota(jnp.int32, (blk, LANES), 1)
    lane_kv = lax.broadcasted_iota(jnp.int32, (2 * blk, LANES), 1)
    col = lax.broadcasted_iota(jnp.int32, (blk, 2 * blk), 1)
    lo_q = lane_q < ATT_HEAD_DIM
    q_lo = jnp.where(lo_q, 1.0, 0.0).astype(BF16)
    q_hi = jnp.where(lo_q, 0.0, 1.0).astype(BF16)
    ind_lo = jnp.where(lane_kv < ATT_HEAD_DIM, 1.0, 0.0).astype(BF16)
    ind_hi = jnp.where(lane_kv < ATT_HEAD_DIM, 0.0, 1.0).astype(BF16)
    no_prev = jnp.where(first_in_seq, -jnp.inf, 0.0).astype(F32)

    items = [(q0, g) for q0 in range(r0, r0 + n_rows, blk) for g in range(ATT_HEADS // 2)]

    def sources(g):
        return ("a", "b") if g < 2 else ("b", "a")

    k_band = {q0: {"a": ka_s[q0:q0 + 2 * blk, :], "b": kb_s[q0:q0 + 2 * blk, :]}
              for q0, _ in items}
    scores = []
    for q0, g in items:
        q2 = q_ref[q0:q0 + blk, g * LANES:(g + 1) * LANES]
        scores.append([_dot_nt(q2 * sel, k_band[q0][src])
                       for sel, src in zip((q_lo, q_hi), sources(g))])

    v_stack = {}
    for q0 in sorted(k_band):
        v_a = va_s[q0:q0 + 2 * blk, :]
        v_b = vb_s[q0:q0 + 2 * blk, :]
        for even, odd, key in ((v_a, v_b, ("a", "b")), (v_b, v_a, ("b", "a"))):
            v_stack[q0, key] = jnp.concatenate(
                [jnp.concatenate([even * ind_lo, ind_lo], axis=1),
                 jnp.concatenate([odd * ind_hi, ind_hi], axis=1)], axis=0)

    probs, sink_terms = [], []
    for (q0, g), pair in zip(items, scores):
        p_pair, t_pair = [], []
        for half, s in enumerate(pair):
            head = 2 * g + half
            s = s + bias_ref[head]
            if q0 == 0:
                s = s + jnp.where(col < blk, no_prev, 0.0)
            sink = sink_ref[head]
            m = jnp.maximum(jnp.max(s, axis=-1, keepdims=True), sink)
            p_pair.append(jnp.exp2(s - m).astype(BF16))
            t_pair.append(jnp.exp2(sink - m))
        probs.append(jnp.concatenate(p_pair, axis=1))
        sink_terms.append(jnp.where(lo_q, t_pair[0], t_pair[1]))
        feeder.feed()

    results = [_dot(p, v_stack[q0, sources(g)]) for (q0, g), p in zip(items, probs)]
    for (q0, g), res, sink_term in zip(items, results, sink_terms):
        den = res[:, LANES:] + sink_term
        a_ref[q0:q0 + blk, g * LANES:(g + 1) * LANES] = (res[:, :LANES] / den).astype(BF16)


def _hgrn_block(r0, q_ref, k_ref, ghi_ref, glo_ref, v_ref, gate_ref, tri2, mask_ref,
                st_ref, r_ref, feeder):
    run_one = feeder.feed
    N = HG_BLOCK
    H = N // 2
    C = HG_CHUNK
    W = q_ref.shape[1]
    rows = slice(r0, r0 + N)
    heads = [slice(h * HG_DK, (h + 1) * HG_DK) for h in range(HG_HEADS)]
    halves = [slice(0, H), slice(H, N)]
    qq = q_ref[rows, :]
    kk = k_ref[rows, :]
    vv = v_ref[rows, :]

    g_pieces = jnp.concatenate([ghi_ref[rows, :], glo_ref[rows, :]], axis=0)
    b = _dot(tri2, g_pieces)
    b_last = b[N - 1:N, :]
    decay = jnp.exp2(b_last)

    def neg_offset(m):
        parts = []
        for k in range(0, N, 2 * m):
            ref = b[k + m - 1:k + m, :]
            parts += [ref - b[k:k + m, :], b[k + m:k + 2 * m, :] - ref]
        return jnp.concatenate(parts, axis=0)

    def finish(atts, cross, o_state):
        probs = [(top.astype(BF16), jnp.concatenate([cross[h], bot], axis=1).astype(BF16))
                 for h, (top, bot) in enumerate(atts)]
        outs = [jnp.concatenate([_dot(p_top, vv[0:H, sl]), _dot(p_bot, vv[:, sl])], axis=0)
                + o_state[h] for h, (sl, (p_top, p_bot)) in enumerate(zip(heads, probs))]
        for sl, o in zip(heads, outs):
            r = o * _rms_scale(o)
            r_ref[rows, sl] = (r * gate_ref[rows, sl].astype(F32)).astype(BF16)

    q_state = qq * jnp.exp2(b).astype(BF16)
    run_one()
    k_state = kk * jnp.exp2(b_last - b).astype(BF16)
    run_one()

    e128 = jnp.exp2(neg_offset(H)).astype(BF16)
    q128 = qq[H:N, :] * e128[H:N, :]
    k128 = kk[0:H, :] * e128[0:H, :]
    run_one()
    e64 = jnp.exp2(neg_offset(C)).astype(BF16)
    q64 = [qq[r + C:r + 2 * C, :] * e64[r + C:r + 2 * C, :] for r in (0, H)]
    k64 = [kk[r:r + C, :] * e64[r:r + C, :] for r in (0, H)]
    run_one()

    so_mid = neg_offset(C // 2)
    span = -jnp.min(so_mid)
    e_mid = jnp.exp2(so_mid)
    small = e_mid.astype(BF16)
    large = (1.0 / e_mid).astype(BF16)
    q_sc, k_sc = [], []
    for k in range(0, N, C):
        q_sc += [large[k:k + C // 2, :], small[k + C // 2:k + C, :]]
        k_sc += [small[k:k + C // 2, :], large[k + C // 2:k + C, :]]
    q_mid = qq * jnp.concatenate(q_sc, axis=0)
    k_mid = kk * jnp.concatenate(k_sc, axis=0)
    feeder.drain()
    in_chunk = mask_ref[0] > 0.5
    pad = jnp.zeros((C, H - C), F32)

    def with_level64(fine, p64):
        return jnp.concatenate(
            [fine[0:C, :], fine[C:H, :] + jnp.concatenate([p64, pad], axis=1)], axis=0)

    mid = [[_dot_nt(q_mid[hv, sl], k_mid[hv, sl]) for hv in halves] for sl in heads]
    lvl64 = [[_dot_nt(q64[i][:, sl], k64[i][:, sl]) for i in range(2)] for sl in heads]
    cross = [_dot_nt(q128[:, sl], k128[:, sl]) for sl in heads]
    o_state = []
    for h, sl in enumerate(heads):
        st = st_ref[h]
        o_state.append(_dot_nt(q_state[:, sl], st.astype(BF16)))
        st_ref[h] = st * decay[:, sl] + _dot_tn(vv[:, sl], k_state[:, sl])
    finish([[with_level64(jnp.where(in_chunk, mid[h][i], 0.0), lvl64[h][i]) for i in range(2)]
            for h in range(HG_HEADS)], cross, o_state)

    @pl.when(span >= HG_SAFE_SPAN)
    def _():
        def offset(m):
            ref = [jnp.broadcast_to(b[k + m - 1:k + m, :], (2 * m, W)) for k in range(0, N, 2 * m)]
            return b - jnp.concatenate(ref, axis=0)

        f = jnp.exp2(ghi_ref[rows, :].astype(F32) + glo_ref[rows, :].astype(F32))
        scales = [e_mid]
        scales += [jnp.exp2(-jnp.abs(offset(m))) for m in HG_LEVELS[1:] if 2 * m >= 8]
        row = lax.broadcasted_iota(jnp.int32, (N, W), 0)
        f_next = pltpu.roll(f, N - 1, 0)
        f_prev = pltpu.roll(f, 1, 0)
        r4 = row & 3
        scales.append(jnp.where(r4 == 0, f_next,
                                jnp.where(r4 == 1, 1.0, jnp.where(r4 == 2, f, f * f_prev))))
        scales.append(jnp.where((row & 1) == 1, f, 1.0))
        pairs = [(qq * e.astype(BF16), kk * e.astype(BF16)) for e in scales] + [(qq, kk)]
        atts = []
        for h, sl in enumerate(heads):
            att = []
            for i, hv in enumerate(halves):
                fine = jnp.zeros((H, H), F32)
                for j, (ql, kl) in enumerate(pairs):
                    fine = fine + _dot_nt(ql[hv, sl], kl[hv, sl]) * mask_ref[1 + j]
                att.append(with_level64(fine, lvl64[h][i]))
            atts.append(att)
        finish(atts, cross, o_state)


def _merge_steps(rows, a_ref, r_ref, sga_ref, sgr_ref, x_ref, wpa_ref, wph_ref, wout_ref, n2,
                 m_ref, x1_ref, h2_ref):
    def gated(cols):
        def run():
            pa = _dot(a_ref[rows, :], wpa_ref[:, cols])
            ph = _dot(r_ref[rows, :], wph_ref[:, cols])
            m_ref[rows, cols] = (sga_ref[rows, cols].astype(F32) * pa
                                 + sgr_ref[rows, cols].astype(F32) * ph).astype(BF16)
        return run

    def projected(cols):
        def run():
            x1_ref[rows, cols] = x_ref[rows, cols] + _dot(m_ref[rows, :], wout_ref[:, cols])
        return run

    def normed():
        x1 = x1_ref[rows, :]
        h2_ref[rows, :] = (x1 * _rms_scale(x1) * n2).astype(BF16)

    tiles = [slice(c, c + MERGE_CW) for c in range(0, D_MODEL, MERGE_CW)]
    return [gated(c) for c in tiles] + [projected(c) for c in tiles] + [normed]


MIX_NAMES = ("qa", "ka", "kb", "va", "vb", "hq", "hk", "ghi", "glo", "hv", "hgate", "sga", "sgr")
MIX_WIDTHS = (ATT_Q, ATT_KV, ATT_KV, ATT_KV, ATT_KV, HG_K, HG_K, HG_K, HG_K, HG_V, HG_V,
              D_MODEL, D_MODEL)
BAND_NAMES = ("ka", "kb", "va", "vb")
SLOTS_PER_BLOCK = 2 * (ATT_HEADS // 2) + 4


def _layer_kernel(sink_ref, x_ref, xlag_ref, n1_ref, w_ref, qg_ref, kg_ref, bd_ref, lb_ref, hn_ref,
                  bias_ref, tri_ref, mask_ref, wpa_ref, wph_ref, wout_ref, n2_ref,
                  x1_ref, h2_ref, *scratch, per_seq):
    h_s, a_s, r_s, m_s, st_ref = scratch[:5]
    sets = dict(zip(MIX_NAMES, scratch[5:]))
    blk = ATT_BLOCK
    tile = x_ref.shape[0]
    i = pl.program_id(0)
    proj_set = lax.rem(i, 2)
    mix_set = 1 - proj_set
    first_in_seq = lax.rem(i - 1, per_seq) == 0

    @pl.when(i == 0)
    def _():
        for ref in sets.values():
            ref[...] = jnp.zeros_like(ref)

    @pl.when(jnp.logical_or(first_in_seq, i == 0))
    def _():
        st_ref[...] = jnp.zeros_like(st_ref)

    for name in BAND_NAMES:
        sets[name][proj_set, 0:blk, :] = sets[name][mix_set, tile:tile + blk, :]

    proj = {name: ref.at[proj_set] for name, ref in sets.items()}
    mix = {name: ref.at[mix_set] for name, ref in sets.items()}
    tri = tri_ref[...]
    n2 = n2_ref[...]
    merge_steps = []
    for r0 in range(0, tile, HG_BLOCK):
        rows = slice(r0, r0 + HG_BLOCK)
        feeder = _Feeder(
            _inproj_steps(rows, x_ref, n1_ref, w_ref, qg_ref, kg_ref, bd_ref, lb_ref, hn_ref,
                          h_s, proj) + merge_steps, SLOTS_PER_BLOCK)
        _attention_rows(r0, HG_BLOCK, first_in_seq, sink_ref, mix["qa"], bias_ref,
                        tuple(mix[name] for name in BAND_NAMES), a_s, feeder)
        _hgrn_block(r0, mix["hq"], mix["hk"], mix["ghi"], mix["glo"], mix["hv"], mix["hgate"],
                    tri, mask_ref, st_ref, r_s, feeder)
        merge_steps = _merge_steps(rows, a_s, r_s, mix["sga"], mix["sgr"], xlag_ref,
                                   wpa_ref, wph_ref, wout_ref, n2, m_s, x1_ref, h2_ref)
    for step in merge_steps:
        step()


def _token_layer(sinks, xf, n1, w_in, qg, kg, bd, lb, hn, bias_tab, wpa, wph, wout, layer, n2, S):
    T = xf.shape[0]
    tm = TM_MIX
    blk = ATT_BLOCK
    n_tiles = T // tm
    masks_np, tri_np = _hgrn_masks()
    masks = jnp.asarray(masks_np, F32)
    tri = jnp.asarray(tri_np, BF16)
    projected = pl.BlockSpec((tm, D_MODEL), lambda i: (jnp.minimum(i, n_tiles - 1), 0))
    mixed = pl.BlockSpec((tm, D_MODEL), lambda i: (jnp.maximum(i - 1, 0), 0))
    operand_sets = [pltpu.VMEM((2, tm + blk if name in BAND_NAMES else tm, width), BF16)
                    for name, width in zip(MIX_NAMES, MIX_WIDTHS)]
    return pl.pallas_call(
        functools.partial(_layer_kernel, per_seq=S // tm),
        grid=(n_tiles + 1,),
        in_specs=[pl.BlockSpec(memory_space=pltpu.SMEM), projected, mixed,
                  _const_spec((1, D_MODEL)), _layer_spec(w_in, layer),
                  _const_spec((1, ATT_Q)), _const_spec((1, ATT_KV)),
                  _const_spec((PROJ_CW, PROJ_CW)), _const_spec((1, HG_K)), _const_spec((1, HG_V)),
                  _const_spec((ATT_HEADS, blk, 2 * blk)),
                  _const_spec(tri.shape), _const_spec(masks.shape),
                  _layer_spec(wpa, layer), _layer_spec(wph, layer), _layer_spec(wout, layer),
                  _const_spec((1, D_MODEL))],
        out_specs=[mixed, mixed],
        out_shape=[jax.ShapeDtypeStruct((T, D_MODEL), F32),
                   jax.ShapeDtypeStruct((T, D_MODEL), BF16)],
        scratch_shapes=[pltpu.VMEM((tm, D_MODEL), BF16), pltpu.VMEM((tm, ATT_Q), BF16),
                        pltpu.VMEM((tm, HG_V), BF16), pltpu.VMEM((tm, D_MODEL), BF16),
                        pltpu.VMEM((HG_HEADS, HG_DV, HG_DK), F32)] + operand_sets,
        compiler_params=_params(),
        name="token_layer",
    )(sinks, xf, xf, n1, w_in, qg, kg, bd, lb, hn, bias_tab, tri, masks, wpa, wph, wout, n2)


def _ffn_kernel(h2_ref, x1_ref, wup_ref, cw_ref, cb_ref, wdn_ref, o_ref,
                carry_ref, act_ref, *, tiles_per_seq):
    tm = h2_ref.shape[0]
    halo = carry_ref.shape[0]

    @pl.when(pl.program_id(0) % tiles_per_seq == 0)
    def _():
        carry_ref[...] = jnp.zeros_like(carry_ref)

    h2 = h2_ref[...]
    head_row = lax.broadcasted_iota(jnp.int32, (halo, FFN_CW), 0)

    def shifted(u, prev, n):
        s = pltpu.roll(u, n, 0)
        head = jnp.where(head_row < n, pltpu.roll(prev, n, 0), s[0:halo, :])
        return jnp.concatenate([head, s[halo:, :]], axis=0)

    def conv(off):
        cols = slice(off, off + FFN_CW)
        u = _dot(h2, wup_ref[:, cols])
        prev = carry_ref[:, cols]
        carry_ref[:, cols] = u[tm - halo:tm, :]
        w = cw_ref[:, cols]
        return (w[0:1, :] * shifted(u, prev, 2) + w[1:2, :] * shifted(u, prev, 1)
                + w[2:3, :] * u + cb_ref[:, cols])

    for c in range(D_FF // FFN_CW):
        ug = conv(c * FFN_CW)
        uv = conv(D_FF + c * FFN_CW)
        act_ref[:, c * FFN_CW:(c + 1) * FFN_CW] = (ug * _sigmoid(ug) * uv).astype(BF16)
    o_ref[...] = x1_ref[...] + _dot(act_ref[...], wdn_ref[...])


def _ffn(h2, x1, wup, cw, cb, wdn, layer, S):
    T = h2.shape[0]
    tm = TM_FFN
    halo = 8
    row = pl.BlockSpec((tm, D_MODEL), lambda i: (i, 0))
    return pl.pallas_call(
        functools.partial(_ffn_kernel, tiles_per_seq=S // tm),
        grid=(T // tm,),
        in_specs=[row, row, _layer_spec(wup, layer),
                  _const_spec((CONV_WIDTH, 2 * D_FF)), _const_spec((1, 2 * D_FF)),
                  _layer_spec(wdn, layer)],
        out_specs=row,
        out_shape=jax.ShapeDtypeStruct((T, D_MODEL), F32),
        scratch_shapes=[pltpu.VMEM((halo, 2 * D_FF), F32),
                        pltpu.VMEM((tm, D_FF), BF16)],
        compiler_params=_params(),
        name="conv_ffn",
    )(h2, x1, wup, cw, cb, wdn)


def _bias_table(rel_bias):
    blk = ATT_BLOCK
    assert WINDOW == blk
    d = np.arange(WINDOW)
    max_exact = N_BUCKETS // 2
    large = max_exact + (np.log(np.maximum(d, 1) / max_exact) / math.log(MAX_DISTANCE / max_exact)
                         * (N_BUCKETS - max_exact)).astype(np.int32)
    bucket = np.where(d < max_exact, d, np.minimum(large, N_BUCKETS - 1))
    onehot = jnp.asarray(bucket[:, None] == np.arange(N_BUCKETS)[None, :], F32)
    per_dist = jnp.dot(onehot, rel_bias.astype(F32), precision=lax.Precision.HIGHEST)
    heads = per_dist.shape[1]
    period = 3 * blk
    z = jnp.concatenate([jnp.full((heads, 1), -jnp.inf, F32), per_dist[::-1].T,
                         jnp.full((heads, period - WINDOW - 1), -jnp.inf, F32)], axis=1)
    skew = jnp.tile(z, (1, blk))[:, :blk * (period - 1)].reshape(heads, blk, period - 1)
    return skew[:, :, :2 * blk]


def _block_diag_ones(n, blk):
    idx = np.arange(n) // blk
    return jnp.asarray((idx[:, None] == idx[None, :]).astype(np.float32), BF16)


def kernel(x, norm1, w_in, q_norm, k_norm, sinks, rel_bias, hg_lb, hg_norm,
           w_pa, w_ph, w_out, norm2, w_up, conv_w, conv_b, w_down):
    B, S, D = x.shape
    T = B * S
    depth = w_in.shape[0]
    assert D == D_MODEL and S % TM_FFN == 0 and S % TM_MIX == 0
    assert TM_MIX % HG_BLOCK == 0 and HG_BLOCK % (2 * ATT_BLOCK) == 0

    p_lb = jax.nn.softmax(hg_lb.astype(F32), axis=0)
    lbs = jnp.cumsum(p_lb, axis=0) - p_lb[0:1]

    log2e = math.log2(math.e)
    bias_tab = _bias_table(rel_bias) * log2e
    bd = _block_diag_ones(PROJ_CW, ATT_HEAD_DIM)
    scale = ATT_HEAD_DIM ** -0.5 * log2e

    w_in, w_pa, w_ph, w_out, w_up, w_down = (
        w.astype(BF16) for w in (w_in, w_pa, w_ph, w_out, w_up, w_down))

    xf = x.reshape(T, D)
    for l in range(depth):
        qg = (jnp.tile(q_norm[l].astype(F32), ATT_HEADS) * scale).reshape(1, ATT_Q)
        kg = jnp.tile(k_norm[l].astype(F32), ATT_KV_HEADS).reshape(1, ATT_KV)
        x1, h2 = _token_layer(
            sinks[l].astype(F32) * log2e, xf, norm1[l].reshape(1, D), w_in, qg, kg, bd,
            lbs[l].reshape(1, HG_K), jnp.tile(hg_norm[l].astype(F32), HG_HEADS).reshape(1, HG_V),
            bias_tab, w_pa, w_ph, w_out, l, norm2[l].reshape(1, D), S)
        xf = _ffn(h2, x1, w_up, conv_w[l].astype(F32),
                  conv_b[l].reshape(1, 2 * D_FF).astype(F32), w_down, l, S)
    return xf.reshape(B, S, D)
```

```python
import functools
import math

import numpy as np
import jax
import jax.numpy as jnp
from jax import lax
from jax.experimental import pallas as pl
from jax.experimental.pallas import tpu as pltpu

F32 = jnp.float32
BF16 = jnp.bfloat16

D_MODEL = 1024
ATT_HEADS = 8
ATT_KV_HEADS = 2
ATT_HEAD_DIM = 64
WINDOW = 128
ATT_BLOCK = 128
N_BUCKETS = 32
MAX_DISTANCE = 128
HG_HEADS = 4
HG_DK = 128
HG_DV = 128
HG_CHUNK = 64
D_FF = 2816
CONV_WIDTH = 3
EPS = 1e-6

ATT_Q = ATT_HEADS * ATT_HEAD_DIM
ATT_KV = ATT_KV_HEADS * ATT_HEAD_DIM
HG_K = HG_HEADS * HG_DK
HG_V = HG_HEADS * HG_DV
IN_SPLITS = (ATT_Q, ATT_KV, ATT_KV, HG_K, HG_K, HG_V, HG_V, D_MODEL, D_MODEL)
IN_OFFS = tuple(int(o) for o in np.cumsum((0,) + IN_SPLITS))
D_IN = IN_OFFS[-1]

LANES = 128
VMEM_LIMIT_BYTES = 56 * 1024 * 1024

TM_PROJ = 512
NORM_GROUP = 256
TM_FFN = 1024
FFN_CW = 256
TM_MIX = 512
MERGE_CW = 256
HG_BLOCK = 256
HG_LEVELS = (32, 16, 8, 4, 2, 1)
HG_SAFE_SPAN = 80.0


def _dot(a, b):
    return jnp.dot(a, b, preferred_element_type=F32)


def _dot_nt(a, b):
    return lax.dot_general(a, b, (((1,), (1,)), ((), ())), preferred_element_type=F32)


def _dot_tn(a, b):
    return lax.dot_general(a, b, (((0,), (0,)), ((), ())), preferred_element_type=F32)


def _sigmoid(x):
    return 1.0 / (1.0 + jnp.exp(-x))


def _rms_scale(x):
    return lax.rsqrt(jnp.mean(x * x, axis=-1, keepdims=True) + EPS)


def _const_spec(shape):
    nd = len(shape)
    return pl.BlockSpec(shape, lambda *_: (0,) * nd, pipeline_mode=pl.Buffered(1))


def _layer_spec(stacked, layer):
    _, rows, cols = stacked.shape
    return pl.BlockSpec((None, rows, cols), lambda *_: (layer, 0, 0),
                        pipeline_mode=pl.Buffered(1))


def _params():
    return pltpu.CompilerParams(vmem_limit_bytes=VMEM_LIMIT_BYTES)


def _inproj_kernel(x_ref, n1_ref, w_ref, qg_ref, kg_ref, bd_ref, lb_ref, hn_ref,
                   qa_ref, ka_ref, kb_ref, va_ref, vb_ref,
                   hq_ref, hk_ref, ghi_ref, glo_ref, hv_ref, hgate_ref, sga_ref, sgr_ref):
    x = x_ref[...]
    h = (x * _rms_scale(x) * n1_ref[...]).astype(BF16)

    def proj(seg):
        return _dot(h, w_ref[:, IN_OFFS[seg]:IN_OFFS[seg + 1]])

    bd = bd_ref[...]

    def head_norm(y, gain, ones):
        sq = (y * y).astype(BF16)
        width = ones.shape[0]
        ms = jnp.concatenate([_dot(sq[:, c:c + width], ones)
                              for c in range(0, y.shape[1], width)], axis=1)
        return y * lax.rsqrt(ms * (1.0 / ATT_HEAD_DIM) + EPS) * gain

    def attn_q(y):
        qa_ref[...] = head_norm(y, qg_ref[...], bd).astype(BF16)

    def attn_k(y):
        kn = head_norm(y, kg_ref[...], bd[0:ATT_KV, 0:ATT_KV])
        ka_ref[...] = kn.astype(BF16)
        kb_ref[...] = pltpu.roll(kn, ATT_HEAD_DIM, 1).astype(BF16)

    def attn_v(y):
        va_ref[...] = y.astype(BF16)
        vb_ref[...] = pltpu.roll(y, ATT_HEAD_DIM, 1).astype(BF16)

    def hgrn_q(y):
        hq_ref[...] = (y * _sigmoid(y)).astype(BF16)

    def hgrn_f(y):
        lb = lb_ref[...]
        f = lb + (1.0 - lb) * _sigmoid(y)
        hk_ref[...] = (1.0 - f).astype(BF16)
        g = jnp.log2(f)
        g_hi = g.astype(BF16)
        ghi_ref[...] = g_hi
        glo_ref[...] = (g - g_hi.astype(F32)).astype(BF16)

    def hgrn_v(y):
        hv_ref[...] = y.astype(BF16)

    def hgrn_gate(y):
        hgate_ref[...] = (y * _sigmoid(y) * hn_ref[...]).astype(BF16)

    def gate_a(y):
        sga_ref[...] = _sigmoid(y).astype(BF16)

    def gate_r(y):
        sgr_ref[...] = _sigmoid(y).astype(BF16)

    order = [(0, attn_q), (3, hgrn_q), (4, hgrn_f), (7, gate_a), (8, gate_r),
             (6, hgrn_gate), (1, attn_k), (2, attn_v), (5, hgrn_v)]
    y = proj(order[0][0])
    for (_, epilogue), (next_seg, _) in zip(order, order[1:]):
        y_next = proj(next_seg)
        epilogue(y)
        y = y_next
    order[-1][1](y)


def _inproj(xf, n1, w_in, layer, qg, kg, bd, lb, hn):
    T = xf.shape[0]
    tm = TM_PROJ
    row = lambda n: pl.BlockSpec((tm, n), lambda i: (i, 0))
    out_widths = [ATT_Q, ATT_KV, ATT_KV, ATT_KV, ATT_KV,
                  HG_K, HG_K, HG_K, HG_K, HG_V, HG_V, D_MODEL, D_MODEL]
    return pl.pallas_call(
        _inproj_kernel,
        grid=(T // tm,),
        in_specs=[row(D_MODEL), _const_spec((1, D_MODEL)), _layer_spec(w_in, layer),
                  _const_spec((1, ATT_Q)), _const_spec((1, ATT_KV)),
                  _const_spec((NORM_GROUP, NORM_GROUP)),
                  _const_spec((1, HG_K)), _const_spec((1, HG_V))],
        out_specs=[row(n) for n in out_widths],
        out_shape=[jax.ShapeDtypeStruct((T, n), BF16) for n in out_widths],
        compiler_params=_params(),
        name="inproj",
    )(xf, n1, w_in, qg, kg, bd, lb, hn)


def _hgrn_masks():
    n = HG_BLOCK // 2
    t = np.arange(n)[:, None]
    s = np.arange(n)[None, :]

    def level(m):
        return ((t // (2 * m)) == (s // (2 * m))) & ((t // m) % 2 == 1) & ((s // m) % 2 == 0)

    masks = [((t // HG_CHUNK) == (s // HG_CHUNK)) & (s <= t)]
    masks += [level(m) for m in HG_LEVELS] + [t == s]
    tri = np.tril(np.ones((HG_BLOCK, HG_BLOCK), np.float32))
    return np.stack(masks, axis=0).astype(np.float32), np.concatenate([tri, tri], axis=1)


def _attention_rows(r0, n_rows, first_in_seq, sink_ref, q_ref, bias_ref, bands, a_ref,
                    between):
    between = list(between)
    blk = ATT_BLOCK
    ka_s, kb_s, va_s, vb_s = bands
    lane_q = lax.broadcasted_iota(jnp.int32, (blk, LANES), 1)
    lane_kv = lax.broadcasted_iota(jnp.int32, (2 * blk, LANES), 1)
    col = lax.broadcasted_iota(jnp.int32, (blk, 2 * blk), 1)
    lo_q = lane_q < ATT_HEAD_DIM
    q_lo = jnp.where(lo_q, 1.0, 0.0).astype(BF16)
    q_hi = jnp.where(lo_q, 0.0, 1.0).astype(BF16)
    ind_lo = jnp.where(lane_kv < ATT_HEAD_DIM, 1.0, 0.0).astype(BF16)
    ind_hi = jnp.where(lane_kv < ATT_HEAD_DIM, 0.0, 1.0).astype(BF16)
    no_prev = jnp.where(first_in_seq, -jnp.inf, 0.0).astype(F32)

    items = [(q0, g) for q0 in range(r0, r0 + n_rows, blk) for g in range(ATT_HEADS // 2)]

    def sources(g):
        return ("a", "b") if g < 2 else ("b", "a")

    k_band = {q0: {"a": ka_s[q0:q0 + 2 * blk, :], "b": kb_s[q0:q0 + 2 * blk, :]}
              for q0, _ in items}
    scores = []
    for q0, g in items:
        q2 = q_ref[q0:q0 + blk, g * LANES:(g + 1) * LANES]
        scores.append([_dot_nt(q2 * sel, k_band[q0][src])
                       for sel, src in zip((q_lo, q_hi), sources(g))])

    v_stack = {}
    for q0 in sorted(k_band):
        v_a = va_s[q0:q0 + 2 * blk, :]
        v_b = vb_s[q0:q0 + 2 * blk, :]
        for even, odd, key in ((v_a, v_b, ("a", "b")), (v_b, v_a, ("b", "a"))):
            v_stack[q0, key] = jnp.concatenate(
                [jnp.concatenate([even * ind_lo, ind_lo], axis=1),
                 jnp.concatenate([odd * ind_hi, ind_hi], axis=1)], axis=0)

    probs, sink_terms = [], []
    for (q0, g), pair in zip(items, scores):
        p_pair, t_pair = [], []
        for half, s in enumerate(pair):
            head = 2 * g + half
            s = s + bias_ref[head]
            if q0 == 0:
                s = s + jnp.where(col < blk, no_prev, 0.0)
            sink = sink_ref[head]
            m = jnp.maximum(jnp.max(s, axis=-1, keepdims=True), sink)
            p_pair.append(jnp.exp2(s - m).astype(BF16))
            t_pair.append(jnp.exp2(sink - m))
        probs.append(jnp.concatenate(p_pair, axis=1))
        sink_terms.append(jnp.where(lo_q, t_pair[0], t_pair[1]))
        if between:
            between.pop(0)()
    for step in between:
        step()

    results = [_dot(p, v_stack[q0, sources(g)]) for (q0, g), p in zip(items, probs)]
    for (q0, g), res, sink_term in zip(items, results, sink_terms):
        den = res[:, LANES:] + sink_term
        a_ref[q0:q0 + blk, g * LANES:(g + 1) * LANES] = (res[:, :LANES] / den).astype(BF16)


def _hgrn_block(r0, q_ref, k_ref, ghi_ref, glo_ref, v_ref, gate_ref, tri2, mask_ref,
                st_ref, r_ref, between):
    between = list(between)

    def run_one():
        if between:
            between.pop(0)()

    N = HG_BLOCK
    H = N // 2
    C = HG_CHUNK
    W = q_ref.shape[1]
    rows = slice(r0, r0 + N)
    heads = [slice(h * HG_DK, (h + 1) * HG_DK) for h in range(HG_HEADS)]
    halves = [slice(0, H), slice(H, N)]
    qq = q_ref[rows, :]
    kk = k_ref[rows, :]
    vv = v_ref[rows, :]

    g_pieces = jnp.concatenate([ghi_ref[rows, :], glo_ref[rows, :]], axis=0)
    b = _dot(tri2, g_pieces)
    b_last = b[N - 1:N, :]
    decay = jnp.exp2(b_last)

    def neg_offset(m):
        parts = []
        for k in range(0, N, 2 * m):
            ref = b[k + m - 1:k + m, :]
            parts += [ref - b[k:k + m, :], b[k + m:k + 2 * m, :] - ref]
        return jnp.concatenate(parts, axis=0)

    def finish(atts, cross, o_state):
        probs = [(top.astype(BF16), jnp.concatenate([cross[h], bot], axis=1).astype(BF16))
                 for h, (top, bot) in enumerate(atts)]
        outs = [jnp.concatenate([_dot(p_top, vv[0:H, sl]), _dot(p_bot, vv[:, sl])], axis=0)
                + o_state[h] for h, (sl, (p_top, p_bot)) in enumerate(zip(heads, probs))]
        for sl, o in zip(heads, outs):
            r = o * _rms_scale(o)
            r_ref[rows, sl] = (r * gate_ref[rows, sl].astype(F32)).astype(BF16)

    q_state = qq * jnp.exp2(b).astype(BF16)
    run_one()
    k_state = kk * jnp.exp2(b_last - b).astype(BF16)
    run_one()

    e128 = jnp.exp2(neg_offset(H)).astype(BF16)
    q128 = qq[H:N, :] * e128[H:N, :]
    k128 = kk[0:H, :] * e128[0:H, :]
    run_one()
    e64 = jnp.exp2(neg_offset(C)).astype(BF16)
    q64 = [qq[r + C:r + 2 * C, :] * e64[r + C:r + 2 * C, :] for r in (0, H)]
    k64 = [kk[r:r + C, :] * e64[r:r + C, :] for r in (0, H)]
    run_one()

    so_mid = neg_offset(C // 2)
    span = -jnp.min(so_mid)
    e_mid = jnp.exp2(so_mid)
    small = e_mid.astype(BF16)
    large = (1.0 / e_mid).astype(BF16)
    q_sc, k_sc = [], []
    for k in range(0, N, C):
        q_sc += [large[k:k + C // 2, :], small[k + C // 2:k + C, :]]
        k_sc += [small[k:k + C // 2, :], large[k + C // 2:k + C, :]]
    q_mid = qq * jnp.concatenate(q_sc, axis=0)
    k_mid = kk * jnp.concatenate(k_sc, axis=0)
    for step in between:
        step()
    in_chunk = mask_ref[0] > 0.5
    pad = jnp.zeros((C, H - C), F32)

    def with_level64(fine, p64):
        return jnp.concatenate(
            [fine[0:C, :], fine[C:H, :] + jnp.concatenate([p64, pad], axis=1)], axis=0)

    mid = [[_dot_nt(q_mid[hv, sl], k_mid[hv, sl]) for hv in halves] for sl in heads]
    lvl64 = [[_dot_nt(q64[i][:, sl], k64[i][:, sl]) for i in range(2)] for sl in heads]
    cross = [_dot_nt(q128[:, sl], k128[:, sl]) for sl in heads]
    o_state = []
    for h, sl in enumerate(heads):
        st = st_ref[h]
        o_state.append(_dot_nt(q_state[:, sl], st.astype(BF16)))
        st_ref[h] = st * decay[:, sl] + _dot_tn(vv[:, sl], k_state[:, sl])
    finish([[with_level64(jnp.where(in_chunk, mid[h][i], 0.0), lvl64[h][i]) for i in range(2)]
            for h in range(HG_HEADS)], cross, o_state)

    @pl.when(span >= HG_SAFE_SPAN)
    def _():
        def offset(m):
            ref = [jnp.broadcast_to(b[k + m - 1:k + m, :], (2 * m, W)) for k in range(0, N, 2 * m)]
            return b - jnp.concatenate(ref, axis=0)

        f = jnp.exp2(ghi_ref[rows, :].astype(F32) + glo_ref[rows, :].astype(F32))
        scales = [e_mid]
        scales += [jnp.exp2(-jnp.abs(offset(m))) for m in HG_LEVELS[1:] if 2 * m >= 8]
        row = lax.broadcasted_iota(jnp.int32, (N, W), 0)
        f_next = pltpu.roll(f, N - 1, 0)
        f_prev = pltpu.roll(f, 1, 0)
        r4 = row & 3
        scales.append(jnp.where(r4 == 0, f_next,
                                jnp.where(r4 == 1, 1.0, jnp.where(r4 == 2, f, f * f_prev))))
        scales.append(jnp.where((row & 1) == 1, f, 1.0))
        pairs = [(qq * e.astype(BF16), kk * e.astype(BF16)) for e in scales] + [(qq, kk)]
        atts = []
        for h, sl in enumerate(heads):
            att = []
            for i, hv in enumerate(halves):
                fine = jnp.zeros((H, H), F32)
                for j, (ql, kl) in enumerate(pairs):
                    fine = fine + _dot_nt(ql[hv, sl], kl[hv, sl]) * mask_ref[1 + j]
                att.append(with_level64(fine, lvl64[h][i]))
            atts.append(att)
        finish(atts, cross, o_state)


def _merge_steps(rows, a_ref, r_ref, sga_ref, sgr_ref, x_ref, wpa_ref, wph_ref, wout_ref, n2,
                 m_ref, x1_ref, h2_ref):
    def gated(cols):
        def run():
            pa = _dot(a_ref[rows, :], wpa_ref[:, cols])
            ph = _dot(r_ref[rows, :], wph_ref[:, cols])
            m_ref[rows, cols] = (sga_ref[rows, cols].astype(F32) * pa
                                 + sgr_ref[rows, cols].astype(F32) * ph).astype(BF16)
        return run

    def projected(cols):
        def run():
            x1_ref[rows, cols] = x_ref[rows, cols] + _dot(m_ref[rows, :], wout_ref[:, cols])
        return run

    def normed():
        x1 = x1_ref[rows, :]
        h2_ref[rows, :] = (x1 * _rms_scale(x1) * n2).astype(BF16)

    tiles = [slice(c, c + MERGE_CW) for c in range(0, D_MODEL, MERGE_CW)]
    return [gated(c) for c in tiles] + [projected(c) for c in tiles] + [normed]


def _mixer_kernel(sink_ref, qa_ref, kac_ref, kbc_ref, vac_ref, vbc_ref,
                  kap_ref, kbp_ref, vap_ref, vbp_ref, bias_ref,
                  hq_ref, hk_ref, ghi_ref, glo_ref, hv_ref, hgate_ref, tri_ref, mask_ref,
                  sga_ref, sgr_ref, x_ref, wpa_ref, wph_ref, wout_ref, n2_ref,
                  x1_ref, h2_ref,
                  ka_s, kb_s, va_s, vb_s, a_s, r_s, m_s, st_ref):
    blk = ATT_BLOCK
    tile = qa_ref.shape[0]
    first_in_seq = pl.program_id(1) == 0

    @pl.when(first_in_seq)
    def _():
        st_ref[...] = jnp.zeros_like(st_ref)

    for s_ref, p_ref, c_ref in ((ka_s, kap_ref, kac_ref), (kb_s, kbp_ref, kbc_ref),
                                (va_s, vap_ref, vac_ref), (vb_s, vbp_ref, vbc_ref)):
        s_ref[0:blk, :] = p_ref[...]
        s_ref[blk:, :] = c_ref[...]

    tri = tri_ref[...]
    n2 = n2_ref[...]
    pending = []
    n_gated = D_MODEL // MERGE_CW
    for r0 in range(0, tile, HG_BLOCK):
        _attention_rows(r0, HG_BLOCK, first_in_seq, sink_ref, qa_ref, bias_ref,
                        (ka_s, kb_s, va_s, vb_s), a_s, pending[:n_gated])
        _hgrn_block(r0, hq_ref, hk_ref, ghi_ref, glo_ref, hv_ref, hgate_ref, tri, mask_ref,
                    st_ref, r_s, pending[n_gated:])
        pending = _merge_steps(slice(r0, r0 + HG_BLOCK), a_s, r_s, sga_ref, sgr_ref, x_ref,
                               wpa_ref, wph_ref, wout_ref, n2, m_s, x1_ref, h2_ref)
    for step in pending:
        step()


def _mixer(sinks, qa, ka, kb, va, vb, bias_tab, hq, hk, ghi, glo, hv, hgate,
           sga, sgr, xf, wpa, wph, wout, layer, n2, B, S):
    T = qa.shape[0]
    tm = TM_MIX
    blk = ATT_BLOCK
    per_seq = S // tm
    ratio = tm // blk
    masks_np, tri_np = _hgrn_masks()
    masks = jnp.asarray(masks_np, F32)
    tri = jnp.asarray(tri_np, BF16)
    cur = lambda n: pl.BlockSpec((tm, n), lambda b, i: (b * per_seq + i, 0))
    prev = pl.BlockSpec(
        (blk, ATT_KV), lambda b, i: (b * (S // blk) + jnp.maximum(i * ratio - 1, 0), 0))
    return pl.pallas_call(
        _mixer_kernel,
        grid=(B, per_seq),
        in_specs=[pl.BlockSpec(memory_space=pltpu.SMEM), cur(ATT_Q),
                  cur(ATT_KV), cur(ATT_KV), cur(ATT_KV), cur(ATT_KV),
                  prev, prev, prev, prev,
                  _const_spec((ATT_HEADS, blk, 2 * blk)),
                  cur(HG_K), cur(HG_K), cur(HG_K), cur(HG_K), cur(HG_V), cur(HG_V),
                  _const_spec(tri.shape), _const_spec(masks.shape),
                  cur(D_MODEL), cur(D_MODEL), cur(D_MODEL),
                  _layer_spec(wpa, layer), _layer_spec(wph, layer), _layer_spec(wout, layer),
                  _const_spec((1, D_MODEL))],
        out_specs=[cur(D_MODEL), cur(D_MODEL)],
        out_shape=[jax.ShapeDtypeStruct((T, D_MODEL), F32),
                   jax.ShapeDtypeStruct((T, D_MODEL), BF16)],
        scratch_shapes=[pltpu.VMEM((tm + blk, ATT_KV), BF16)] * 4
        + [pltpu.VMEM((tm, ATT_Q), BF16), pltpu.VMEM((tm, HG_V), BF16),
           pltpu.VMEM((tm, D_MODEL), BF16), pltpu.VMEM((HG_HEADS, HG_DV, HG_DK), F32)],
        compiler_params=_params(),
        name="mixer",
    )(sinks, qa, ka, kb, va, vb, ka, kb, va, vb, bias_tab,
      hq, hk, ghi, glo, hv, hgate, tri, masks, sga, sgr, xf, wpa, wph, wout, n2)


def _ffn_kernel(h2_ref, x1_ref, wup_ref, cw_ref, cb_ref, wdn_ref, o_ref,
                carry_ref, act_ref, *, tiles_per_seq):
    tm = h2_ref.shape[0]
    halo = carry_ref.shape[0]

    @pl.when(pl.program_id(0) % tiles_per_seq == 0)
    def _():
        carry_ref[...] = jnp.zeros_like(carry_ref)

    h2 = h2_ref[...]
    head_row = lax.broadcasted_iota(jnp.int32, (halo, FFN_CW), 0)

    def shifted(u, prev, n):
        s = pltpu.roll(u, n, 0)
        head = jnp.where(head_row < n, pltpu.roll(prev, n, 0), s[0:halo, :])
        return jnp.concatenate([head, s[halo:, :]], axis=0)

    def up(c):
        return [_dot(h2, wup_ref[:, off:off + FFN_CW]) for off in (c * FFN_CW, D_FF + c * FFN_CW)]

    def conv(u, off):
        cols = slice(off, off + FFN_CW)
        prev = carry_ref[:, cols]
        carry_ref[:, cols] = u[tm - halo:tm, :]
        w = cw_ref[:, cols]
        return (w[0:1, :] * shifted(u, prev, 2) + w[1:2, :] * shifted(u, prev, 1)
                + w[2:3, :] * u + cb_ref[:, cols])

    n_steps = D_FF // FFN_CW
    u_gate, u_val = up(0)
    for c in range(n_steps):
        ahead = up(c + 1) if c + 1 < n_steps else None
        ug = conv(u_gate, c * FFN_CW)
        uv = conv(u_val, D_FF + c * FFN_CW)
        act_ref[:, c * FFN_CW:(c + 1) * FFN_CW] = (ug * _sigmoid(ug) * uv).astype(BF16)
        if ahead is not None:
            u_gate, u_val = ahead
    o_ref[...] = x1_ref[...] + _dot(act_ref[...], wdn_ref[...])


def _ffn(h2, x1, wup, cw, cb, wdn, layer, S):
    T = h2.shape[0]
    tm = TM_FFN
    halo = 8
    row = pl.BlockSpec((tm, D_MODEL), lambda i: (i, 0))
    return pl.pallas_call(
        functools.partial(_ffn_kernel, tiles_per_seq=S // tm),
        grid=(T // tm,),
        in_specs=[row, row, _layer_spec(wup, layer),
                  _const_spec((CONV_WIDTH, 2 * D_FF)), _const_spec((1, 2 * D_FF)),
                  _layer_spec(wdn, layer)],
        out_specs=row,
        out_shape=jax.ShapeDtypeStruct((T, D_MODEL), F32),
        scratch_shapes=[pltpu.VMEM((halo, 2 * D_FF), F32),
                        pltpu.VMEM((tm, D_FF), BF16)],
        compiler_params=_params(),
        name="conv_ffn",
    )(h2, x1, wup, cw, cb, wdn)


def _bias_table(rel_bias):
    blk = ATT_BLOCK
    assert WINDOW == blk
    d = np.arange(WINDOW)
    max_exact = N_BUCKETS // 2
    large = max_exact + (np.log(np.maximum(d, 1) / max_exact) / math.log(MAX_DISTANCE / max_exact)
                         * (N_BUCKETS - max_exact)).astype(np.int32)
    bucket = np.where(d < max_exact, d, np.minimum(large, N_BUCKETS - 1))
    onehot = jnp.asarray(bucket[:, None] == np.arange(N_BUCKETS)[None, :], F32)
    per_dist = jnp.dot(onehot, rel_bias.astype(F32), precision=lax.Precision.HIGHEST)
    heads = per_dist.shape[1]
    period = 3 * blk
    z = jnp.concatenate([jnp.full((heads, 1), -jnp.inf, F32), per_dist[::-1].T,
                         jnp.full((heads, period - WINDOW - 1), -jnp.inf, F32)], axis=1)
    skew = jnp.tile(z, (1, blk))[:, :blk * (period - 1)].reshape(heads, blk, period - 1)
    return skew[:, :, :2 * blk]


def _block_diag_ones(n, blk):
    idx = np.arange(n) // blk
    return jnp.asarray((idx[:, None] == idx[None, :]).astype(np.float32), BF16)


def kernel(x, norm1, w_in, q_norm, k_norm, sinks, rel_bias, hg_lb, hg_norm,
           w_pa, w_ph, w_out, norm2, w_up, conv_w, conv_b, w_down):
    B, S, D = x.shape
    T = B * S
    depth = w_in.shape[0]
    assert D == D_MODEL and T % TM_PROJ == 0 and S % TM_FFN == 0 and S % TM_MIX == 0
    assert TM_MIX % HG_BLOCK == 0 and HG_BLOCK % (2 * ATT_BLOCK) == 0

    p_lb = jax.nn.softmax(hg_lb.astype(F32), axis=0)
    lbs = jnp.cumsum(p_lb, axis=0) - p_lb[0:1]

    log2e = math.log2(math.e)
    bias_tab = _bias_table(rel_bias) * log2e
    bd = _block_diag_ones(NORM_GROUP, ATT_HEAD_DIM)
    scale = ATT_HEAD_DIM ** -0.5 * log2e

    w_in, w_pa, w_ph, w_out, w_up, w_down = (
        w.astype(BF16) for w in (w_in, w_pa, w_ph, w_out, w_up, w_down))

    xf = x.reshape(T, D)
    for l in range(depth):
        qg = (jnp.tile(q_norm[l].astype(F32), ATT_HEADS) * scale).reshape(1, ATT_Q)
        kg = jnp.tile(k_norm[l].astype(F32), ATT_KV_HEADS).reshape(1, ATT_KV)
        (qa, ka, kb, va, vb, hq, hk, ghi, glo, hv, hgate, sga, sgr) = _inproj(
            xf, norm1[l].reshape(1, D), w_in, l, qg, kg, bd,
            lbs[l].reshape(1, HG_K),
            jnp.tile(hg_norm[l].astype(F32), HG_HEADS).reshape(1, HG_V))
        x1, h2 = _mixer(sinks[l].astype(F32) * log2e, qa, ka, kb, va, vb, bias_tab,
                        hq, hk, ghi, glo, hv, hgate,
                        sga, sgr, xf, w_pa, w_ph, w_out, l, norm2[l].reshape(1, D), B, S)
        xf = _ffn(h2, x1, w_up, conv_w[l].astype(F32),
                  conv_b[l].reshape(1, 2 * D_FF).astype(F32), w_down, l, S)
    return xf.reshape(B, S, D)
```

```python
import functools
import math

import numpy as np
import jax
import jax.numpy as jnp
from jax import lax
from jax.experimental import pallas as pl
from jax.experimental.pallas import tpu as pltpu

F32 = jnp.float32
BF16 = jnp.bfloat16

D_MODEL = 1024
ATT_HEADS = 8
ATT_KV_HEADS = 2
ATT_HEAD_DIM = 64
WINDOW = 128
ATT_BLOCK = 128
N_BUCKETS = 32
MAX_DISTANCE = 128
HG_HEADS = 4
HG_DK = 128
HG_DV = 128
HG_CHUNK = 64
D_FF = 2816
CONV_WIDTH = 3
EPS = 1e-6

ATT_Q = ATT_HEADS * ATT_HEAD_DIM
ATT_KV = ATT_KV_HEADS * ATT_HEAD_DIM
HG_K = HG_HEADS * HG_DK
HG_V = HG_HEADS * HG_DV
IN_SPLITS = (ATT_Q, ATT_KV, ATT_KV, HG_K, HG_K, HG_V, HG_V, D_MODEL, D_MODEL)
IN_OFFS = tuple(int(o) for o in np.cumsum((0,) + IN_SPLITS))
D_IN = IN_OFFS[-1]

LANES = 128
VMEM_LIMIT_BYTES = 56 * 1024 * 1024

TM_PROJ = 512
NORM_GROUP = 256
TM_FFN = 1024
FFN_CW = 256
TM_MIX = 512
MERGE_CW = 256
HG_BLOCK = 256
HG_LEVELS = (32, 16, 8, 4, 2, 1)
HG_SAFE_SPAN = 80.0


def _dot(a, b):
    return jnp.dot(a, b, preferred_element_type=F32)


def _dot_nt(a, b):
    return lax.dot_general(a, b, (((1,), (1,)), ((), ())), preferred_element_type=F32)


def _dot_tn(a, b):
    return lax.dot_general(a, b, (((0,), (0,)), ((), ())), preferred_element_type=F32)


def _sigmoid(x):
    return 1.0 / (1.0 + jnp.exp(-x))


def _rms_scale(x):
    return lax.rsqrt(jnp.mean(x * x, axis=-1, keepdims=True) + EPS)


def _const_spec(shape):
    nd = len(shape)
    return pl.BlockSpec(shape, lambda *_: (0,) * nd, pipeline_mode=pl.Buffered(1))


def _layer_spec(stacked, layer):
    _, rows, cols = stacked.shape
    return pl.BlockSpec((None, rows, cols), lambda *_: (layer, 0, 0),
                        pipeline_mode=pl.Buffered(1))


def _params():
    return pltpu.CompilerParams(vmem_limit_bytes=VMEM_LIMIT_BYTES)


def _inproj_kernel(x_ref, n1_ref, w_ref, qg_ref, kg_ref, bd_ref, lb_ref, hn_ref,
                   qa_ref, ka_ref, kb_ref, va_ref, vb_ref,
                   hq_ref, hk_ref, ghi_ref, glo_ref, hv_ref, hgate_ref, sga_ref, sgr_ref):
    x = x_ref[...]
    h = (x * _rms_scale(x) * n1_ref[...]).astype(BF16)

    def proj(seg):
        return _dot(h, w_ref[:, IN_OFFS[seg]:IN_OFFS[seg + 1]])

    bd = bd_ref[...]

    def head_norm(y, gain, ones):
        sq = (y * y).astype(BF16)
        width = ones.shape[0]
        ms = jnp.concatenate([_dot(sq[:, c:c + width], ones)
                              for c in range(0, y.shape[1], width)], axis=1)
        return y * lax.rsqrt(ms * (1.0 / ATT_HEAD_DIM) + EPS) * gain

    def attn_q(y):
        qa_ref[...] = head_norm(y, qg_ref[...], bd).astype(BF16)

    def attn_k(y):
        kn = head_norm(y, kg_ref[...], bd[0:ATT_KV, 0:ATT_KV])
        ka_ref[...] = kn.astype(BF16)
        kb_ref[...] = pltpu.roll(kn, ATT_HEAD_DIM, 1).astype(BF16)

    def attn_v(y):
        va_ref[...] = y.astype(BF16)
        vb_ref[...] = pltpu.roll(y, ATT_HEAD_DIM, 1).astype(BF16)

    def hgrn_q(y):
        hq_ref[...] = (y * _sigmoid(y)).astype(BF16)

    def hgrn_f(y):
        lb = lb_ref[...]
        f = lb + (1.0 - lb) * _sigmoid(y)
        hk_ref[...] = (1.0 - f).astype(BF16)
        g = jnp.log2(f)
        g_hi = g.astype(BF16)
        ghi_ref[...] = g_hi
        glo_ref[...] = (g - g_hi.astype(F32)).astype(BF16)

    def hgrn_v(y):
        hv_ref[...] = y.astype(BF16)

    def hgrn_gate(y):
        hgate_ref[...] = (y * _sigmoid(y) * hn_ref[...]).astype(BF16)

    def gate_a(y):
        sga_ref[...] = _sigmoid(y).astype(BF16)

    def gate_r(y):
        sgr_ref[...] = _sigmoid(y).astype(BF16)

    order = [(0, attn_q), (3, hgrn_q), (4, hgrn_f), (7, gate_a), (8, gate_r),
             (6, hgrn_gate), (1, attn_k), (2, attn_v), (5, hgrn_v)]
    y = proj(order[0][0])
    for (_, epilogue), (next_seg, _) in zip(order, order[1:]):
        y_next = proj(next_seg)
        epilogue(y)
        y = y_next
    order[-1][1](y)


def _inproj(xf, n1, w_in, layer, qg, kg, bd, lb, hn):
    T = xf.shape[0]
    tm = TM_PROJ
    row = lambda n: pl.BlockSpec((tm, n), lambda i: (i, 0))
    out_widths = [ATT_Q, ATT_KV, ATT_KV, ATT_KV, ATT_KV,
                  HG_K, HG_K, HG_K, HG_K, HG_V, HG_V, D_MODEL, D_MODEL]
    return pl.pallas_call(
        _inproj_kernel,
        grid=(T // tm,),
        in_specs=[row(D_MODEL), _const_spec((1, D_MODEL)), _layer_spec(w_in, layer),
                  _const_spec((1, ATT_Q)), _const_spec((1, ATT_KV)),
                  _const_spec((NORM_GROUP, NORM_GROUP)),
                  _const_spec((1, HG_K)), _const_spec((1, HG_V))],
        out_specs=[row(n) for n in out_widths],
        out_shape=[jax.ShapeDtypeStruct((T, n), BF16) for n in out_widths],
        compiler_params=_params(),
        name="inproj",
    )(xf, n1, w_in, qg, kg, bd, lb, hn)


def _hgrn_masks():
    n = HG_BLOCK // 2
    t = np.arange(n)[:, None]
    s = np.arange(n)[None, :]

    def level(m):
        return ((t // (2 * m)) == (s // (2 * m))) & ((t // m) % 2 == 1) & ((s // m) % 2 == 0)

    masks = [((t // HG_CHUNK) == (s // HG_CHUNK)) & (s <= t)]
    masks += [level(m) for m in HG_LEVELS] + [t == s]
    tri = np.tril(np.ones((HG_BLOCK, HG_BLOCK), np.float32))
    return np.stack(masks, axis=0).astype(np.float32), np.concatenate([tri, tri], axis=1)


def _attention_rows(r0, n_rows, first_in_seq, sink_ref, q_ref, bias_ref, bands, a_ref,
                    between):
    between = list(between)
    blk = ATT_BLOCK
    ka_s, kb_s, va_s, vb_s = bands
    lane_q = lax.broadcasted_iota(jnp.int32, (blk, LANES), 1)
    lane_kv = lax.broadcasted_iota(jnp.int32, (2 * blk, LANES), 1)
    col = lax.broadcasted_iota(jnp.int32, (blk, 2 * blk), 1)
    lo_q = lane_q < ATT_HEAD_DIM
    q_lo = jnp.where(lo_q, 1.0, 0.0).astype(BF16)
    q_hi = jnp.where(lo_q, 0.0, 1.0).astype(BF16)
    ind_lo = jnp.where(lane_kv < ATT_HEAD_DIM, 1.0, 0.0).astype(BF16)
    ind_hi = jnp.where(lane_kv < ATT_HEAD_DIM, 0.0, 1.0).astype(BF16)
    no_prev = jnp.where(first_in_seq, -jnp.inf, 0.0).astype(F32)

    items = [(q0, g) for q0 in range(r0, r0 + n_rows, blk) for g in range(ATT_HEADS // 2)]

    def sources(g):
        return ("a", "b") if g < 2 else ("b", "a")

    k_band = {q0: {"a": ka_s[q0:q0 + 2 * blk, :], "b": kb_s[q0:q0 + 2 * blk, :]}
              for q0, _ in items}
    head_scores = {}
    for q0 in sorted(k_band):
        for src in ("a", "b"):
            members = [(g, half) for g in range(ATT_HEADS // 2) for half in (0, 1)
                       if sources(g)[half] == src]
            lhs = jnp.concatenate(
                [q_ref[q0:q0 + blk, g * LANES:(g + 1) * LANES] * (q_hi if half else q_lo)
                 for g, half in members], axis=0)
            stacked = _dot_nt(lhs, k_band[q0][src])
            for i, member in enumerate(members):
                head_scores[(q0,) + member] = stacked[i * blk:(i + 1) * blk, :]
    scores = [[head_scores[q0, g, half] for half in (0, 1)] for q0, g in items]

    v_stack = {}
    for q0 in sorted(k_band):
        v_a = va_s[q0:q0 + 2 * blk, :]
        v_b = vb_s[q0:q0 + 2 * blk, :]
        for even, odd, key in ((v_a, v_b, ("a", "b")), (v_b, v_a, ("b", "a"))):
            v_stack[q0, key] = jnp.concatenate(
                [jnp.concatenate([even * ind_lo, ind_lo], axis=1),
                 jnp.concatenate([odd * ind_hi, ind_hi], axis=1)], axis=0)

    probs, sink_terms = [], []
    for (q0, g), pair in zip(items, scores):
        p_pair, t_pair = [], []
        for half, s in enumerate(pair):
            head = 2 * g + half
            s = s + bias_ref[head]
            if q0 == 0:
                s = s + jnp.where(col < blk, no_prev, 0.0)
            sink = sink_ref[head]
            m = jnp.maximum(jnp.max(s, axis=-1, keepdims=True), sink)
            p_pair.append(jnp.exp2(s - m).astype(BF16))
            t_pair.append(jnp.exp2(sink - m))
        probs.append(jnp.concatenate(p_pair, axis=1))
        sink_terms.append(jnp.where(lo_q, t_pair[0], t_pair[1]))
        if between:
            between.pop(0)()
    for step in between:
        step()

    by_item = dict(zip(items, probs))
    results = {}
    for q0, key in v_stack:
        members = [g for g in range(ATT_HEADS // 2) if sources(g) == key]
        stacked = _dot(jnp.concatenate([by_item[q0, g] for g in members], axis=0),
                       v_stack[q0, key])
        for i, g in enumerate(members):
            results[q0, g] = stacked[i * blk:(i + 1) * blk, :]
    results = [results[item] for item in items]
    for (q0, g), res, sink_term in zip(items, results, sink_terms):
        den = res[:, LANES:] + sink_term
        a_ref[q0:q0 + blk, g * LANES:(g + 1) * LANES] = (res[:, :LANES] / den).astype(BF16)


def _hgrn_block(r0, q_ref, k_ref, ghi_ref, glo_ref, v_ref, gate_ref, tri2, mask_ref,
                st_ref, r_ref, between):
    between = list(between)

    def run_one():
        if between:
            between.pop(0)()

    N = HG_BLOCK
    H = N // 2
    C = HG_CHUNK
    W = q_ref.shape[1]
    rows = slice(r0, r0 + N)
    heads = [slice(h * HG_DK, (h + 1) * HG_DK) for h in range(HG_HEADS)]
    halves = [slice(0, H), slice(H, N)]
    qq = q_ref[rows, :]
    kk = k_ref[rows, :]
    vv = v_ref[rows, :]

    g_pieces = jnp.concatenate([ghi_ref[rows, :], glo_ref[rows, :]], axis=0)
    b = _dot(tri2, g_pieces)
    b_last = b[N - 1:N, :]
    decay = jnp.exp2(b_last)

    def neg_offset(m):
        parts = []
        for k in range(0, N, 2 * m):
            ref = b[k + m - 1:k + m, :]
            parts += [ref - b[k:k + m, :], b[k + m:k + 2 * m, :] - ref]
        return jnp.concatenate(parts, axis=0)

    def finish(atts, cross, o_state):
        probs = [(top.astype(BF16), jnp.concatenate([cross[h], bot], axis=1).astype(BF16))
                 for h, (top, bot) in enumerate(atts)]
        outs = [jnp.concatenate([_dot(p_top, vv[0:H, sl]), _dot(p_bot, vv[:, sl])], axis=0)
                + o_state[h] for h, (sl, (p_top, p_bot)) in enumerate(zip(heads, probs))]
        for sl, o in zip(heads, outs):
            r = o * _rms_scale(o)
            r_ref[rows, sl] = (r * gate_ref[rows, sl].astype(F32)).astype(BF16)

    q_state = qq * jnp.exp2(b).astype(BF16)
    run_one()
    k_state = kk * jnp.exp2(b_last - b).astype(BF16)
    run_one()

    e128 = jnp.exp2(neg_offset(H)).astype(BF16)
    q128 = qq[H:N, :] * e128[H:N, :]
    k128 = kk[0:H, :] * e128[0:H, :]
    run_one()
    e64 = jnp.exp2(neg_offset(C)).astype(BF16)
    q64 = [qq[r + C:r + 2 * C, :] * e64[r + C:r + 2 * C, :] for r in (0, H)]
    k64 = [kk[r:r + C, :] * e64[r:r + C, :] for r in (0, H)]
    run_one()

    so_mid = neg_offset(C // 2)
    span = -jnp.min(so_mid)
    e_mid = jnp.exp2(so_mid)
    small = e_mid.astype(BF16)
    large = (1.0 / e_mid).astype(BF16)
    q_sc, k_sc = [], []
    for k in range(0, N, C):
        q_sc += [large[k:k + C // 2, :], small[k + C // 2:k + C, :]]
        k_sc += [small[k:k + C // 2, :], large[k + C // 2:k + C, :]]
    q_mid = qq * jnp.concatenate(q_sc, axis=0)
    k_mid = kk * jnp.concatenate(k_sc, axis=0)
    for step in between:
        step()
    in_chunk = mask_ref[0] > 0.5
    pad = jnp.zeros((C, H - C), F32)

    def with_level64(fine, p64):
        return jnp.concatenate(
            [fine[0:C, :], fine[C:H, :] + jnp.concatenate([p64, pad], axis=1)], axis=0)

    mid = [[_dot_nt(q_mid[hv, sl], k_mid[hv, sl]) for hv in halves] for sl in heads]
    lvl64 = [[_dot_nt(q64[i][:, sl], k64[i][:, sl]) for i in range(2)] for sl in heads]
    cross = [_dot_nt(q128[:, sl], k128[:, sl]) for sl in heads]
    o_state = []
    for h, sl in enumerate(heads):
        st = st_ref[h]
        o_state.append(_dot_nt(q_state[:, sl], st.astype(BF16)))
        st_ref[h] = st * decay[:, sl] + _dot_tn(vv[:, sl], k_state[:, sl])
    finish([[with_level64(jnp.where(in_chunk, mid[h][i], 0.0), lvl64[h][i]) for i in range(2)]
            for h in range(HG_HEADS)], cross, o_state)

    @pl.when(span >= HG_SAFE_SPAN)
    def _():
        def offset(m):
            ref = [jnp.broadcast_to(b[k + m - 1:k + m, :], (2 * m, W)) for k in range(0, N, 2 * m)]
            return b - jnp.concatenate(ref, axis=0)

        f = jnp.exp2(ghi_ref[rows, :].astype(F32) + glo_ref[rows, :].astype(F32))
        scales = [e_mid]
        scales += [jnp.exp2(-jnp.abs(offset(m))) for m in HG_LEVELS[1:] if 2 * m >= 8]
        row = lax.broadcasted_iota(jnp.int32, (N, W), 0)
        f_next = pltpu.roll(f, N - 1, 0)
        f_prev = pltpu.roll(f, 1, 0)
        r4 = row & 3
        scales.append(jnp.where(r4 == 0, f_next,
                                jnp.where(r4 == 1, 1.0, jnp.where(r4 == 2, f, f * f_prev))))
        scales.append(jnp.where((row & 1) == 1, f, 1.0))
        pairs = [(qq * e.astype(BF16), kk * e.astype(BF16)) for e in scales] + [(qq, kk)]
        atts = []
        for h, sl in enumerate(heads):
            att = []
            for i, hv in enumerate(halves):
                fine = jnp.zeros((H, H), F32)
                for j, (ql, kl) in enumerate(pairs):
                    fine = fine + _dot_nt(ql[hv, sl], kl[hv, sl]) * mask_ref[1 + j]
                att.append(with_level64(fine, lvl64[h][i]))
            atts.append(att)
        finish(atts, cross, o_state)


def _merge_steps(rows, a_ref, r_ref, sga_ref, sgr_ref, x_ref, wpa_ref, wph_ref, wout_ref, n2,
                 m_ref, x1_ref, h2_ref):
    def gated(cols):
        def run():
            pa = _dot(a_ref[rows, :], wpa_ref[:, cols])
            ph = _dot(r_ref[rows, :], wph_ref[:, cols])
            m_ref[rows, cols] = (sga_ref[rows, cols].astype(F32) * pa
                                 + sgr_ref[rows, cols].astype(F32) * ph).astype(BF16)
        return run

    def projected(cols):
        def run():
            x1_ref[rows, cols] = x_ref[rows, cols] + _dot(m_ref[rows, :], wout_ref[:, cols])
        return run

    def normed():
        x1 = x1_ref[rows, :]
        h2_ref[rows, :] = (x1 * _rms_scale(x1) * n2).astype(BF16)

    tiles = [slice(c, c + MERGE_CW) for c in range(0, D_MODEL, MERGE_CW)]
    return [gated(c) for c in tiles] + [projected(c) for c in tiles] + [normed]


def _mixer_kernel(sink_ref, qa_ref, kac_ref, kbc_ref, vac_ref, vbc_ref,
                  kap_ref, kbp_ref, vap_ref, vbp_ref, bias_ref,
                  hq_ref, hk_ref, ghi_ref, glo_ref, hv_ref, hgate_ref, tri_ref, mask_ref,
                  sga_ref, sgr_ref, x_ref, wpa_ref, wph_ref, wout_ref, n2_ref,
                  x1_ref, h2_ref,
                  ka_s, kb_s, va_s, vb_s, a_s, r_s, m_s, st_ref):
    blk = ATT_BLOCK
    tile = qa_ref.shape[0]
    first_in_seq = pl.program_id(1) == 0

    @pl.when(first_in_seq)
    def _():
        st_ref[...] = jnp.zeros_like(st_ref)

    for s_ref, p_ref, c_ref in ((ka_s, kap_ref, kac_ref), (kb_s, kbp_ref, kbc_ref),
                                (va_s, vap_ref, vac_ref), (vb_s, vbp_ref, vbc_ref)):
        s_ref[0:blk, :] = p_ref[...]
        s_ref[blk:, :] = c_ref[...]

    tri = tri_ref[...]
    n2 = n2_ref[...]
    pending = []
    n_gated = D_MODEL // MERGE_CW
    for r0 in range(0, tile, HG_BLOCK):
        _attention_rows(r0, HG_BLOCK, first_in_seq, sink_ref, qa_ref, bias_ref,
                        (ka_s, kb_s, va_s, vb_s), a_s, pending[:n_gated])
        _hgrn_block(r0, hq_ref, hk_ref, ghi_ref, glo_ref, hv_ref, hgate_ref, tri, mask_ref,
                    st_ref, r_s, pending[n_gated:])
        pending = _merge_steps(slice(r0, r0 + HG_BLOCK), a_s, r_s, sga_ref, sgr_ref, x_ref,
                               wpa_ref, wph_ref, wout_ref, n2, m_s, x1_ref, h2_ref)
    for step in pending:
        step()


def _mixer(sinks, qa, ka, kb, va, vb, bias_tab, hq, hk, ghi, glo, hv, hgate,
           sga, sgr, xf, wpa, wph, wout, layer, n2, B, S):
    T = qa.shape[0]
    tm = TM_MIX
    blk = ATT_BLOCK
    per_seq = S // tm
    ratio = tm // blk
    masks_np, tri_np = _hgrn_masks()
    masks = jnp.asarray(masks_np, F32)
    tri = jnp.asarray(tri_np, BF16)
    cur = lambda n: pl.BlockSpec((tm, n), lambda b, i: (b * per_seq + i, 0))
    prev = pl.BlockSpec(
        (blk, ATT_KV), lambda b, i: (b * (S // blk) + jnp.maximum(i * ratio - 1, 0), 0))
    return pl.pallas_call(
        _mixer_kernel,
        grid=(B, per_seq),
        in_specs=[pl.BlockSpec(memory_space=pltpu.SMEM), cur(ATT_Q),
                  cur(ATT_KV), cur(ATT_KV), cur(ATT_KV), cur(ATT_KV),
                  prev, prev, prev, prev,
                  _const_spec((ATT_HEADS, blk, 2 * blk)),
                  cur(HG_K), cur(HG_K), cur(HG_K), cur(HG_K), cur(HG_V), cur(HG_V),
                  _const_spec(tri.shape), _const_spec(masks.shape),
                  cur(D_MODEL), cur(D_MODEL), cur(D_MODEL),
                  _layer_spec(wpa, layer), _layer_spec(wph, layer), _layer_spec(wout, layer),
                  _const_spec((1, D_MODEL))],
        out_specs=[cur(D_MODEL), cur(D_MODEL)],
        out_shape=[jax.ShapeDtypeStruct((T, D_MODEL), F32),
                   jax.ShapeDtypeStruct((T, D_MODEL), BF16)],
        scratch_shapes=[pltpu.VMEM((tm + blk, ATT_KV), BF16)] * 4
        + [pltpu.VMEM((tm, ATT_Q), BF16), pltpu.VMEM((tm, HG_V), BF16),
           pltpu.VMEM((tm, D_MODEL), BF16), pltpu.VMEM((HG_HEADS, HG_DV, HG_DK), F32)],
        compiler_params=_params(),
        name="mixer",
    )(sinks, qa, ka, kb, va, vb, ka, kb, va, vb, bias_tab,
      hq, hk, ghi, glo, hv, hgate, tri, masks, sga, sgr, xf, wpa, wph, wout, n2)


def _ffn_kernel(h2_ref, x1_ref, wup_ref, cw_ref, cb_ref, wdn_ref, o_ref,
                carry_ref, act_ref, *, tiles_per_seq):
    tm = h2_ref.shape[0]
    halo = carry_ref.shape[0]

    @pl.when(pl.program_id(0) % tiles_per_seq == 0)
    def _():
        carry_ref[...] = jnp.zeros_like(carry_ref)

    h2 = h2_ref[...]
    head_row = lax.broadcasted_iota(jnp.int32, (halo, FFN_CW), 0)

    def shifted(u, prev, n):
        s = pltpu.roll(u, n, 0)
        head = jnp.where(head_row < n, pltpu.roll(prev, n, 0), s[0:halo, :])
        return jnp.concatenate([head, s[halo:, :]], axis=0)

    def up(c):
        return [_dot(h2, wup_ref[:, off:off + FFN_CW]) for off in (c * FFN_CW, D_FF + c * FFN_CW)]

    def conv(u, off):
        cols = slice(off, off + FFN_CW)
        prev = carry_ref[:, cols]
        carry_ref[:, cols] = u[tm - halo:tm, :]
        w = cw_ref[:, cols]
        return (w[0:1, :] * shifted(u, prev, 2) + w[1:2, :] * shifted(u, prev, 1)
                + w[2:3, :] * u + cb_ref[:, cols])

    n_steps = D_FF // FFN_CW
    u_gate, u_val = up(0)
    for c in range(n_steps):
        ahead = up(c + 1) if c + 1 < n_steps else None
        ug = conv(u_gate, c * FFN_CW)
        uv = conv(u_val, D_FF + c * FFN_CW)
        act_ref[:, c * FFN_CW:(c + 1) * FFN_CW] = (ug * _sigmoid(ug) * uv).astype(BF16)
        if ahead is not None:
            u_gate, u_val = ahead
    o_ref[...] = x1_ref[...] + _dot(act_ref[...], wdn_ref[...])


def _ffn(h2, x1, wup, cw, cb, wdn, layer, S):
    T = h2.shape[0]
    tm = TM_FFN
    halo = 8
    row = pl.BlockSpec((tm, D_MODEL), lambda i: (i, 0))
    return pl.pallas_call(
        functools.partial(_ffn_kernel, tiles_per_seq=S // tm),
        grid=(T // tm,),
        in_specs=[row, row, _layer_spec(wup, layer),
                  _const_spec((CONV_WIDTH, 2 * D_FF)), _const_spec((1, 2 * D_FF)),
                  _layer_spec(wdn, layer)],
        out_specs=row,
        out_shape=jax.ShapeDtypeStruct((T, D_MODEL), F32),
        scratch_shapes=[pltpu.VMEM((halo, 2 * D_FF), F32),
                        pltpu.VMEM((tm, D_FF), BF16)],
        compiler_params=_params(),
        name="conv_ffn",
    )(h2, x1, wup, cw, cb, wdn)


def _bias_table(rel_bias):
    blk = ATT_BLOCK
    assert WINDOW == blk
    d = np.arange(WINDOW)
    max_exact = N_BUCKETS // 2
    large = max_exact + (np.log(np.maximum(d, 1) / max_exact) / math.log(MAX_DISTANCE / max_exact)
                         * (N_BUCKETS - max_exact)).astype(np.int32)
    bucket = np.where(d < max_exact, d, np.minimum(large, N_BUCKETS - 1))
    onehot = jnp.asarray(bucket[:, None] == np.arange(N_BUCKETS)[None, :], F32)
    per_dist = jnp.dot(onehot, rel_bias.astype(F32), precision=lax.Precision.HIGHEST)
    heads = per_dist.shape[1]
    period = 3 * blk
    z = jnp.concatenate([jnp.full((heads, 1), -jnp.inf, F32), per_dist[::-1].T,
                         jnp.full((heads, period - WINDOW - 1), -jnp.inf, F32)], axis=1)
    skew = jnp.tile(z, (1, blk))[:, :blk * (period - 1)].reshape(heads, blk, period - 1)
    return skew[:, :, :2 * blk]


def _block_diag_ones(n, blk):
    idx = np.arange(n) // blk
    return jnp.asarray((idx[:, None] == idx[None, :]).astype(np.float32), BF16)


def kernel(x, norm1, w_in, q_norm, k_norm, sinks, rel_bias, hg_lb, hg_norm,
           w_pa, w_ph, w_out, norm2, w_up, conv_w, conv_b, w_down):
    B, S, D = x.shape
    T = B * S
    depth = w_in.shape[0]
    assert D == D_MODEL and T % TM_PROJ == 0 and S % TM_FFN == 0 and S % TM_MIX == 0
    assert TM_MIX % HG_BLOCK == 0 and HG_BLOCK % (2 * ATT_BLOCK) == 0

    p_lb = jax.nn.softmax(hg_lb.astype(F32), axis=0)
    lbs = jnp.cumsum(p_lb, axis=0) - p_lb[0:1]

    log2e = math.log2(math.e)
    bias_tab = _bias_table(rel_bias) * log2e
    bd = _block_diag_ones(NORM_GROUP, ATT_HEAD_DIM)
    scale = ATT_HEAD_DIM ** -0.5 * log2e

    w_in, w_pa, w_ph, w_out, w_up, w_down = (
        w.astype(BF16) for w in (w_in, w_pa, w_ph, w_out, w_up, w_down))

    xf = x.reshape(T, D)
    for l in range(depth):
        qg = (jnp.tile(q_norm[l].astype(F32), ATT_HEADS) * scale).reshape(1, ATT_Q)
        kg = jnp.tile(k_norm[l].astype(F32), ATT_KV_HEADS).reshape(1, ATT_KV)
        (qa, ka, kb, va, vb, hq, hk, ghi, glo, hv, hgate, sga, sgr) = _inproj(
            xf, norm1[l].reshape(1, D), w_in, l, qg, kg, bd,
            lbs[l].reshape(1, HG_K),
            jnp.tile(hg_norm[l].astype(F32), HG_HEADS).reshape(1, HG_V))
        x1, h2 = _mixer(sinks[l].astype(F32) * log2e, qa, ka, kb, va, vb, bias_tab,
                        hq, hk, ghi, glo, hv, hgate,
                        sga, sgr, xf, w_pa, w_ph, w_out, l, norm2[l].reshape(1, D), B, S)
        xf = _ffn(h2, x1, w_up, conv_w[l].astype(F32),
                  conv_b[l].reshape(1, 2 * D_FF).astype(F32), w_down, l, S)
    return xf.reshape(B, S, D)
```

```python
import functools
import math

import numpy as np
import jax
import jax.numpy as jnp
from jax import lax
from jax.experimental import pallas as pl
from jax.experimental.pallas import tpu as pltpu

F32 = jnp.float32
BF16 = jnp.bfloat16

D_MODEL = 1024
ATT_HEADS = 8
ATT_KV_HEADS = 2
ATT_HEAD_DIM = 64
WINDOW = 128
ATT_BLOCK = 128
N_BUCKETS = 32
MAX_DISTANCE = 128
HG_HEADS = 4
HG_DK = 128
HG_DV = 128
HG_CHUNK = 64
D_FF = 2816
CONV_WIDTH = 3
EPS = 1e-6

ATT_Q = ATT_HEADS * ATT_HEAD_DIM
ATT_KV = ATT_KV_HEADS * ATT_HEAD_DIM
HG_K = HG_HEADS * HG_DK
HG_V = HG_HEADS * HG_DV
IN_SPLITS = (ATT_Q, ATT_KV, ATT_KV, HG_K, HG_K, HG_V, HG_V, D_MODEL, D_MODEL)
IN_OFFS = tuple(int(o) for o in np.cumsum((0,) + IN_SPLITS))
D_IN = IN_OFFS[-1]

LANES = 128
VMEM_LIMIT_BYTES = 56 * 1024 * 1024

TM_PROJ = 512
NORM_GROUP = 256
TM_FFN = 1024
FFN_CW = 256
TM_MIX = 512
MERGE_CW = 256
HG_BLOCK = 256
HG_LEVELS = (32, 16, 8, 4, 2, 1)
HG_SAFE_SPAN = 80.0


def _dot(a, b):
    return jnp.dot(a, b, preferred_element_type=F32)


def _dot_nt(a, b):
    return lax.dot_general(a, b, (((1,), (1,)), ((), ())), preferred_element_type=F32)


def _dot_tn(a, b):
    return lax.dot_general(a, b, (((0,), (0,)), ((), ())), preferred_element_type=F32)


def _sigmoid(x):
    return 1.0 / (1.0 + jnp.exp(-x))


def _rms_scale(x):
    return lax.rsqrt(jnp.mean(x * x, axis=-1, keepdims=True) + EPS)


def _const_spec(shape):
    nd = len(shape)
    return pl.BlockSpec(shape, lambda *_: (0,) * nd, pipeline_mode=pl.Buffered(1))


def _layer_spec(stacked, layer):
    _, rows, cols = stacked.shape
    return pl.BlockSpec((None, rows, cols), lambda *_: (layer, 0, 0),
                        pipeline_mode=pl.Buffered(1))


def _params():
    return pltpu.CompilerParams(vmem_limit_bytes=VMEM_LIMIT_BYTES)


def _inproj_kernel(x_ref, n1_ref, w_ref, qg_ref, kg_ref, bd_ref, lb_ref, hn_ref,
                   qa_ref, ka_ref, kb_ref, va_ref, vb_ref,
                   hq_ref, hk_ref, ghi_ref, glo_ref, hv_ref, hgate_ref, sga_ref, sgr_ref):
    x = x_ref[...]
    h = (x * _rms_scale(x) * n1_ref[...]).astype(BF16)

    def proj(seg):
        return _dot(h, w_ref[:, IN_OFFS[seg]:IN_OFFS[seg + 1]])

    bd = bd_ref[...]

    def head_norm(y, gain, ones):
        sq = (y * y).astype(BF16)
        width = ones.shape[0]
        ms = jnp.concatenate([_dot(sq[:, c:c + width], ones)
                              for c in range(0, y.shape[1], width)], axis=1)
        return y * lax.rsqrt(ms * (1.0 / ATT_HEAD_DIM) + EPS) * gain

    def attn_q(y):
        qa_ref[...] = head_norm(y, qg_ref[...], bd).astype(BF16)

    def attn_k(y):
        kn = head_norm(y, kg_ref[...], bd[0:ATT_KV, 0:ATT_KV])
        ka_ref[...] = kn.astype(BF16)
        kb_ref[...] = pltpu.roll(kn, ATT_HEAD_DIM, 1).astype(BF16)

    def attn_v(y):
        va_ref[...] = y.astype(BF16)
        vb_ref[...] = pltpu.roll(y, ATT_HEAD_DIM, 1).astype(BF16)

    def hgrn_q(y):
        hq_ref[...] = (y * _sigmoid(y)).astype(BF16)

    def hgrn_f(y):
        lb = lb_ref[...]
        f = lb + (1.0 - lb) * _sigmoid(y)
        hk_ref[...] = (1.0 - f).astype(BF16)
        g = jnp.log2(f)
        g_hi = g.astype(BF16)
        ghi_ref[...] = g_hi
        glo_ref[...] = (g - g_hi.astype(F32)).astype(BF16)

    def hgrn_v(y):
        hv_ref[...] = y.astype(BF16)

    def hgrn_gate(y):
        hgate_ref[...] = (y * _sigmoid(y) * hn_ref[...]).astype(BF16)

    def gate_a(y):
        sga_ref[...] = _sigmoid(y).astype(BF16)

    def gate_r(y):
        sgr_ref[...] = _sigmoid(y).astype(BF16)

    order = [(0, attn_q), (3, hgrn_q), (4, hgrn_f), (7, gate_a), (8, gate_r),
             (6, hgrn_gate), (1, attn_k), (2, attn_v), (5, hgrn_v)]
    y = proj(order[0][0])
    for (_, epilogue), (next_seg, _) in zip(order, order[1:]):
        y_next = proj(next_seg)
        epilogue(y)
        y = y_next
    order[-1][1](y)


def _inproj(xf, n1, w_in, layer, qg, kg, bd, lb, hn):
    T = xf.shape[0]
    tm = TM_PROJ
    row = lambda n: pl.BlockSpec((tm, n), lambda i: (i, 0))
    out_widths = [ATT_Q, ATT_KV, ATT_KV, ATT_KV, ATT_KV,
                  HG_K, HG_K, HG_K, HG_K, HG_V, HG_V, D_MODEL, D_MODEL]
    return pl.pallas_call(
        _inproj_kernel,
        grid=(T // tm,),
        in_specs=[row(D_MODEL), _const_spec((1, D_MODEL)), _layer_spec(w_in, layer),
                  _const_spec((1, ATT_Q)), _const_spec((1, ATT_KV)),
                  _const_spec((NORM_GROUP, NORM_GROUP)),
                  _const_spec((1, HG_K)), _const_spec((1, HG_V))],
        out_specs=[row(n) for n in out_widths],
        out_shape=[jax.ShapeDtypeStruct((T, n), BF16) for n in out_widths],
        compiler_params=_params(),
        name="inproj",
    )(xf, n1, w_in, qg, kg, bd, lb, hn)


def _hgrn_masks():
    n = HG_BLOCK // 2
    t = np.arange(n)[:, None]
    s = np.arange(n)[None, :]

    def level(m):
        return ((t // (2 * m)) == (s // (2 * m))) & ((t // m) % 2 == 1) & ((s // m) % 2 == 0)

    masks = [((t // HG_CHUNK) == (s // HG_CHUNK)) & (s <= t)]
    masks += [level(m) for m in HG_LEVELS] + [t == s]
    tri = np.tril(np.ones((n, n), np.float32))
    return np.stack(masks, axis=0).astype(np.float32), np.concatenate([tri, tri], axis=1)


def _attention_rows(r0, n_rows, first_in_seq, sink_ref, q_ref, bias_ref, bands, a_ref,
                    between):
    between = list(between)
    blk = ATT_BLOCK
    ka_s, kb_s, va_s, vb_s = bands
    lane_q = lax.broadcasted_iota(jnp.int32, (blk, LANES), 1)
    lane_kv = lax.broadcasted_iota(jnp.int32, (2 * blk, LANES), 1)
    col = lax.broadcasted_iota(jnp.int32, (blk, 2 * blk), 1)
    lo_q = lane_q < ATT_HEAD_DIM
    q_lo = jnp.where(lo_q, 1.0, 0.0).astype(BF16)
    q_hi = jnp.where(lo_q, 0.0, 1.0).astype(BF16)
    ind_lo = jnp.where(lane_kv < ATT_HEAD_DIM, 1.0, 0.0).astype(BF16)
    ind_hi = jnp.where(lane_kv < ATT_HEAD_DIM, 0.0, 1.0).astype(BF16)
    no_prev = jnp.where(first_in_seq, -jnp.inf, 0.0).astype(F32)

    items = [(q0, g) for q0 in range(r0, r0 + n_rows, blk) for g in range(ATT_HEADS // 2)]

    def sources(g):
        return ("a", "b") if g < 2 else ("b", "a")

    k_band = {q0: {"a": ka_s[q0:q0 + 2 * blk, :], "b": kb_s[q0:q0 + 2 * blk, :]}
              for q0, _ in items}
    head_scores = {}
    for q0 in sorted(k_band):
        for src in ("a", "b"):
            members = [(g, half) for g in range(ATT_HEADS // 2) for half in (0, 1)
                       if sources(g)[half] == src]
            lhs = jnp.concatenate(
                [q_ref[q0:q0 + blk, g * LANES:(g + 1) * LANES] * (q_hi if half else q_lo)
                 for g, half in members], axis=0)
            stacked = _dot_nt(lhs, k_band[q0][src])
            for i, member in enumerate(members):
                head_scores[(q0,) + member] = stacked[i * blk:(i + 1) * blk, :]
    scores = [[head_scores[q0, g, half] for half in (0, 1)] for q0, g in items]

    v_stack = {}
    for q0 in sorted(k_band):
        v_a = va_s[q0:q0 + 2 * blk, :]
        v_b = vb_s[q0:q0 + 2 * blk, :]
        for even, odd, key in ((v_a, v_b, ("a", "b")), (v_b, v_a, ("b", "a"))):
            v_stack[q0, key] = jnp.concatenate(
                [jnp.concatenate([even * ind_lo, ind_lo], axis=1),
                 jnp.concatenate([odd * ind_hi, ind_hi], axis=1)], axis=0)

    probs, sink_terms = [], []
    for (q0, g), pair in zip(items, scores):
        p_pair, t_pair = [], []
        for half, s in enumerate(pair):
            head = 2 * g + half
            s = s + bias_ref[head]
            if q0 == 0:
                s = s + jnp.where(col < blk, no_prev, 0.0)
            sink = sink_ref[head]
            m = jnp.maximum(jnp.max(s, axis=-1, keepdims=True), sink)
            p_pair.append(jnp.exp2(s - m).astype(BF16))
            t_pair.append(jnp.exp2(sink - m))
        probs.append(jnp.concatenate(p_pair, axis=1))
        sink_terms.append(jnp.where(lo_q, t_pair[0], t_pair[1]))
        if between:
            between.pop(0)()
    for step in between:
        step()

    by_item = dict(zip(items, probs))
    results = {}
    for q0, key in v_stack:
        members = [g for g in range(ATT_HEADS // 2) if sources(g) == key]
        stacked = _dot(jnp.concatenate([by_item[q0, g] for g in members], axis=0),
                       v_stack[q0, key])
        for i, g in enumerate(members):
            results[q0, g] = stacked[i * blk:(i + 1) * blk, :]
    results = [results[item] for item in items]
    for (q0, g), res, sink_term in zip(items, results, sink_terms):
        den = res[:, LANES:] + sink_term
        a_ref[q0:q0 + blk, g * LANES:(g + 1) * LANES] = (res[:, :LANES] / den).astype(BF16)


def _hgrn_block(r0, q_ref, k_ref, ghi_ref, glo_ref, v_ref, gate_ref, tri2, mask_ref,
                st_ref, r_ref, between):
    between = list(between)

    def run_one():
        if between:
            between.pop(0)()

    N = HG_BLOCK
    H = N // 2
    C = HG_CHUNK
    W = q_ref.shape[1]
    rows = slice(r0, r0 + N)
    heads = [slice(h * HG_DK, (h + 1) * HG_DK) for h in range(HG_HEADS)]
    halves = [slice(0, H), slice(H, N)]
    qq = q_ref[rows, :]
    kk = k_ref[rows, :]
    vv = v_ref[rows, :]

    def half_cumsum(r):
        pieces = [ref[r0 + r:r0 + r + H, :] for ref in (ghi_ref, glo_ref)]
        return _dot(tri2, jnp.concatenate(pieces, axis=0))

    b_top = half_cumsum(0)
    b = jnp.concatenate([b_top, half_cumsum(H) + b_top[H - 1:H, :]], axis=0)
    b_last = b[N - 1:N, :]
    decay = jnp.exp2(b_last)

    def neg_offset(m):
        parts = []
        for k in range(0, N, 2 * m):
            ref = b[k + m - 1:k + m, :]
            parts += [ref - b[k:k + m, :], b[k + m:k + 2 * m, :] - ref]
        return jnp.concatenate(parts, axis=0)

    def finish(atts, cross, o_state):
        probs = [(top.astype(BF16), jnp.concatenate([cross[h], bot], axis=1).astype(BF16))
                 for h, (top, bot) in enumerate(atts)]
        outs = [jnp.concatenate([_dot(p_top, vv[0:H, sl]), _dot(p_bot, vv[:, sl])], axis=0)
                + o_state[h] for h, (sl, (p_top, p_bot)) in enumerate(zip(heads, probs))]
        for sl, o in zip(heads, outs):
            r = o * _rms_scale(o)
            r_ref[rows, sl] = (r * gate_ref[rows, sl].astype(F32)).astype(BF16)

    q_state = qq * jnp.exp2(b).astype(BF16)
    run_one()
    k_state = kk * jnp.exp2(b_last - b).astype(BF16)
    run_one()

    e128 = jnp.exp2(neg_offset(H)).astype(BF16)
    q128 = qq[H:N, :] * e128[H:N, :]
    k128 = kk[0:H, :] * e128[0:H, :]
    run_one()
    e64 = jnp.exp2(neg_offset(C)).astype(BF16)
    q64 = [qq[r + C:r + 2 * C, :] * e64[r + C:r + 2 * C, :] for r in (0, H)]
    k64 = [kk[r:r + C, :] * e64[r:r + C, :] for r in (0, H)]
    run_one()

    so_mid = neg_offset(C // 2)
    span = -jnp.min(so_mid)
    e_mid = jnp.exp2(so_mid)
    small = e_mid.astype(BF16)
    large = (1.0 / e_mid).astype(BF16)
    q_sc, k_sc = [], []
    for k in range(0, N, C):
        q_sc += [large[k:k + C // 2, :], small[k + C // 2:k + C, :]]
        k_sc += [small[k:k + C // 2, :], large[k + C // 2:k + C, :]]
    q_mid = qq * jnp.concatenate(q_sc, axis=0)
    k_mid = kk * jnp.concatenate(k_sc, axis=0)
    for step in between:
        step()
    in_chunk = mask_ref[0] > 0.5
    pad = jnp.zeros((C, H - C), F32)

    def with_level64(fine, p64):
        return jnp.concatenate(
            [fine[0:C, :], fine[C:H, :] + jnp.concatenate([p64, pad], axis=1)], axis=0)

    mid = [[_dot_nt(q_mid[hv, sl], k_mid[hv, sl]) for hv in halves] for sl in heads]
    lvl64 = [[_dot_nt(q64[i][:, sl], k64[i][:, sl]) for i in range(2)] for sl in heads]
    cross = [_dot_nt(q128[:, sl], k128[:, sl]) for sl in heads]
    o_state = []
    for h, sl in enumerate(heads):
        st = st_ref[h]
        o_state.append(_dot_nt(q_state[:, sl], st.astype(BF16)))
        st_ref[h] = st * decay[:, sl] + _dot_tn(vv[:, sl], k_state[:, sl])
    finish([[with_level64(jnp.where(in_chunk, mid[h][i], 0.0), lvl64[h][i]) for i in range(2)]
            for h in range(HG_HEADS)], cross, o_state)

    @pl.when(span >= HG_SAFE_SPAN)
    def _():
        def offset(m):
            ref = [jnp.broadcast_to(b[k + m - 1:k + m, :], (2 * m, W)) for k in range(0, N, 2 * m)]
            return b - jnp.concatenate(ref, axis=0)

        f = jnp.exp2(ghi_ref[rows, :].astype(F32) + glo_ref[rows, :].astype(F32))
        scales = [e_mid]
        scales += [jnp.exp2(-jnp.abs(offset(m))) for m in HG_LEVELS[1:] if 2 * m >= 8]
        row = lax.broadcasted_iota(jnp.int32, (N, W), 0)
        f_next = pltpu.roll(f, N - 1, 0)
        f_prev = pltpu.roll(f, 1, 0)
        r4 = row & 3
        scales.append(jnp.where(r4 == 0, f_next,
                                jnp.where(r4 == 1, 1.0, jnp.where(r4 == 2, f, f * f_prev))))
        scales.append(jnp.where((row & 1) == 1, f, 1.0))
        pairs = [(qq * e.astype(BF16), kk * e.astype(BF16)) for e in scales] + [(qq, kk)]
        atts = []
        for h, sl in enumerate(heads):
            att = []
            for i, hv in enumerate(halves):
                fine = jnp.zeros((H, H), F32)
                for j, (ql, kl) in enumerate(pairs):
                    fine = fine + _dot_nt(ql[hv, sl], kl[hv, sl]) * mask_ref[1 + j]
                att.append(with_level64(fine, lvl64[h][i]))
            atts.append(att)
        finish(atts, cross, o_state)


def _merge_steps(rows, a_ref, r_ref, sga_ref, sgr_ref, x_ref, wpa_ref, wph_ref, wout_ref, n2,
                 m_ref, x1_ref, h2_ref):
    def gated(cols):
        def run():
            pa = _dot(a_ref[rows, :], wpa_ref[:, cols])
            ph = _dot(r_ref[rows, :], wph_ref[:, cols])
            m_ref[rows, cols] = (sga_ref[rows, cols].astype(F32) * pa
                                 + sgr_ref[rows, cols].astype(F32) * ph).astype(BF16)
        return run

    def projected(cols):
        def run():
            x1_ref[rows, cols] = x_ref[rows, cols] + _dot(m_ref[rows, :], wout_ref[:, cols])
        return run

    def normed():
        x1 = x1_ref[rows, :]
        h2_ref[rows, :] = (x1 * _rms_scale(x1) * n2).astype(BF16)

    tiles = [slice(c, c + MERGE_CW) for c in range(0, D_MODEL, MERGE_CW)]
    return [gated(c) for c in tiles] + [projected(c) for c in tiles] + [normed]


def _mixer_kernel(sink_ref, qa_ref, kac_ref, kbc_ref, vac_ref, vbc_ref,
                  kap_ref, kbp_ref, vap_ref, vbp_ref, bias_ref,
                  hq_ref, hk_ref, ghi_ref, glo_ref, hv_ref, hgate_ref, tri_ref, mask_ref,
                  sga_ref, sgr_ref, x_ref, wpa_ref, wph_ref, wout_ref, n2_ref,
                  x1_ref, h2_ref,
                  ka_s, kb_s, va_s, vb_s, a_s, r_s, m_s, st_ref):
    blk = ATT_BLOCK
    tile = qa_ref.shape[0]
    first_in_seq = pl.program_id(1) == 0

    @pl.when(first_in_seq)
    def _():
        st_ref[...] = jnp.zeros_like(st_ref)

    for s_ref, p_ref, c_ref in ((ka_s, kap_ref, kac_ref), (kb_s, kbp_ref, kbc_ref),
                                (va_s, vap_ref, vac_ref), (vb_s, vbp_ref, vbc_ref)):
        s_ref[0:blk, :] = p_ref[...]
        s_ref[blk:, :] = c_ref[...]

    tri = tri_ref[...]
    n2 = n2_ref[...]
    pending = []
    n_gated = D_MODEL // MERGE_CW
    for r0 in range(0, tile, HG_BLOCK):
        _attention_rows(r0, HG_BLOCK, first_in_seq, sink_ref, qa_ref, bias_ref,
                        (ka_s, kb_s, va_s, vb_s), a_s, pending[:n_gated])
        _hgrn_block(r0, hq_ref, hk_ref, ghi_ref, glo_ref, hv_ref, hgate_ref, tri, mask_ref,
                    st_ref, r_s, pending[n_gated:])
        pending = _merge_steps(slice(r0, r0 + HG_BLOCK), a_s, r_s, sga_ref, sgr_ref, x_ref,
                               wpa_ref, wph_ref, wout_ref, n2, m_s, x1_ref, h2_ref)
    for step in pending:
        step()


def _mixer(sinks, qa, ka, kb, va, vb, bias_tab, hq, hk, ghi, glo, hv, hgate,
           sga, sgr, xf, wpa, wph, wout, layer, n2, B, S):
    T = qa.shape[0]
    tm = TM_MIX
    blk = ATT_BLOCK
    per_seq = S // tm
    ratio = tm // blk
    masks_np, tri_np = _hgrn_masks()
    masks = jnp.asarray(masks_np, F32)
    tri = jnp.asarray(tri_np, BF16)
    cur = lambda n: pl.BlockSpec((tm, n), lambda b, i: (b * per_seq + i, 0))
    prev = pl.BlockSpec(
        (blk, ATT_KV), lambda b, i: (b * (S // blk) + jnp.maximum(i * ratio - 1, 0), 0))
    return pl.pallas_call(
        _mixer_kernel,
        grid=(B, per_seq),
        in_specs=[pl.BlockSpec(memory_space=pltpu.SMEM), cur(ATT_Q),
                  cur(ATT_KV), cur(ATT_KV), cur(ATT_KV), cur(ATT_KV),
                  prev, prev, prev, prev,
                  _const_spec((ATT_HEADS, blk, 2 * blk)),
                  cur(HG_K), cur(HG_K), cur(HG_K), cur(HG_K), cur(HG_V), cur(HG_V),
                  _const_spec(tri.shape), _const_spec(masks.shape),
                  cur(D_MODEL), cur(D_MODEL), cur(D_MODEL),
                  _layer_spec(wpa, layer), _layer_spec(wph, layer), _layer_spec(wout, layer),
                  _const_spec((1, D_MODEL))],
        out_specs=[cur(D_MODEL), cur(D_MODEL)],
        out_shape=[jax.ShapeDtypeStruct((T, D_MODEL), F32),
                   jax.ShapeDtypeStruct((T, D_MODEL), BF16)],
        scratch_shapes=[pltpu.VMEM((tm + blk, ATT_KV), BF16)] * 4
        + [pltpu.VMEM((tm, ATT_Q), BF16), pltpu.VMEM((tm, HG_V), BF16),
           pltpu.VMEM((tm, D_MODEL), BF16), pltpu.VMEM((HG_HEADS, HG_DV, HG_DK), F32)],
        compiler_params=_params(),
        name="mixer",
    )(sinks, qa, ka, kb, va, vb, ka, kb, va, vb, bias_tab,
      hq, hk, ghi, glo, hv, hgate, tri, masks, sga, sgr, xf, wpa, wph, wout, n2)


def _ffn_kernel(h2_ref, x1_ref, wup_ref, cw_ref, cb_ref, wdn_ref, o_ref,
                carry_ref, act_ref, *, tiles_per_seq):
    tm = h2_ref.shape[0]
    halo = carry_ref.shape[0]

    @pl.when(pl.program_id(0) % tiles_per_seq == 0)
    def _():
        carry_ref[...] = jnp.zeros_like(carry_ref)

    h2 = h2_ref[...]
    head_row = lax.broadcasted_iota(jnp.int32, (halo, FFN_CW), 0)

    def shifted(u, prev, n):
        s = pltpu.roll(u, n, 0)
        head = jnp.where(head_row < n, pltpu.roll(prev, n, 0), s[0:halo, :])
        return jnp.concatenate([head, s[halo:, :]], axis=0)

    def up(c):
        return [_dot(h2, wup_ref[:, off:off + FFN_CW]) for off in (c * FFN_CW, D_FF + c * FFN_CW)]

    def conv(u, off):
        cols = slice(off, off + FFN_CW)
        prev = carry_ref[:, cols]
        carry_ref[:, cols] = u[tm - halo:tm, :]
        w = cw_ref[:, cols]
        return (w[0:1, :] * shifted(u, prev, 2) + w[1:2, :] * shifted(u, prev, 1)
                + w[2:3, :] * u + cb_ref[:, cols])

    n_steps = D_FF // FFN_CW
    u_gate, u_val = up(0)
    for c in range(n_steps):
        ahead = up(c + 1) if c + 1 < n_steps else None
        ug = conv(u_gate, c * FFN_CW)
        uv = conv(u_val, D_FF + c * FFN_CW)
        act_ref[:, c * FFN_CW:(c + 1) * FFN_CW] = (ug * _sigmoid(ug) * uv).astype(BF16)
        if ahead is not None:
            u_gate, u_val = ahead
    o_ref[...] = x1_ref[...] + _dot(act_ref[...], wdn_ref[...])


def _ffn(h2, x1, wup, cw, cb, wdn, layer, S):
    T = h2.shape[0]
    tm = TM_FFN
    halo = 8
    row = pl.BlockSpec((tm, D_MODEL), lambda i: (i, 0))
    return pl.pallas_call(
        functools.partial(_ffn_kernel, tiles_per_seq=S // tm),
        grid=(T // tm,),
        in_specs=[row, row, _layer_spec(wup, layer),
                  _const_spec((CONV_WIDTH, 2 * D_FF)), _const_spec((1, 2 * D_FF)),
                  _layer_spec(wdn, layer)],
        out_specs=row,
        out_shape=jax.ShapeDtypeStruct((T, D_MODEL), F32),
        scratch_shapes=[pltpu.VMEM((halo, 2 * D_FF), F32),
                        pltpu.VMEM((tm, D_FF), BF16)],
        compiler_params=_params(),
        name="conv_ffn",
    )(h2, x1, wup, cw, cb, wdn)


def _bias_table(rel_bias):
    blk = ATT_BLOCK
    assert WINDOW == blk
    d = np.arange(WINDOW)
    max_exact = N_BUCKETS // 2
    large = max_exact + (np.log(np.maximum(d, 1) / max_exact) / math.log(MAX_DISTANCE / max_exact)
                         * (N_BUCKETS - max_exact)).astype(np.int32)
    bucket = np.where(d < max_exact, d, np.minimum(large, N_BUCKETS - 1))
    onehot = jnp.asarray(bucket[:, None] == np.arange(N_BUCKETS)[None, :], F32)
    per_dist = jnp.dot(onehot, rel_bias.astype(F32), precision=lax.Precision.HIGHEST)
    heads = per_dist.shape[1]
    period = 3 * blk
    z = jnp.concatenate([jnp.full((heads, 1), -jnp.inf, F32), per_dist[::-1].T,
                         jnp.full((heads, period - WINDOW - 1), -jnp.inf, F32)], axis=1)
    skew = jnp.tile(z, (1, blk))[:, :blk * (period - 1)].reshape(heads, blk, period - 1)
    return skew[:, :, :2 * blk]


def _block_diag_ones(n, blk):
    idx = np.arange(n) // blk
    return jnp.asarray((idx[:, None] == idx[None, :]).astype(np.float32), BF16)


def kernel(x, norm1, w_in, q_norm, k_norm, sinks, rel_bias, hg_lb, hg_norm,
           w_pa, w_ph, w_out, norm2, w_up, conv_w, conv_b, w_down):
    B, S, D = x.shape
    T = B * S
    depth = w_in.shape[0]
    assert D == D_MODEL and T % TM_PROJ == 0 and S % TM_FFN == 0 and S % TM_MIX == 0
    assert TM_MIX % HG_BLOCK == 0 and HG_BLOCK % (2 * ATT_BLOCK) == 0

    p_lb = jax.nn.softmax(hg_lb.astype(F32), axis=0)
    lbs = jnp.cumsum(p_lb, axis=0) - p_lb[0:1]

    log2e = math.log2(math.e)
    bias_tab = _bias_table(rel_bias) * log2e
    bd = _block_diag_ones(NORM_GROUP, ATT_HEAD_DIM)
    scale = ATT_HEAD_DIM ** -0.5 * log2e

    w_in, w_pa, w_ph, w_out, w_up, w_down = (
        w.astype(BF16) for w in (w_in, w_pa, w_ph, w_out, w_up, w_down))

    xf = x.reshape(T, D)
    for l in range(depth):
        qg = (jnp.tile(q_norm[l].astype(F32), ATT_HEADS) * scale).reshape(1, ATT_Q)
        kg = jnp.tile(k_norm[l].astype(F32), ATT_KV_HEADS).reshape(1, ATT_KV)
        (qa, ka, kb, va, vb, hq, hk, ghi, glo, hv, hgate, sga, sgr) = _inproj(
            xf, norm1[l].reshape(1, D), w_in, l, qg, kg, bd,
            lbs[l].reshape(1, HG_K),
            jnp.tile(hg_norm[l].astype(F32), HG_HEADS).reshape(1, HG_V))
        x1, h2 = _mixer(sinks[l].astype(F32) * log2e, qa, ka, kb, va, vb, bias_tab,
                        hq, hk, ghi, glo, hv, hgate,
                        sga, sgr, xf, w_pa, w_ph, w_out, l, norm2[l].reshape(1, D), B, S)
        xf = _ffn(h2, x1, w_up, conv_w[l].astype(F32),
                  conv_b[l].reshape(1, 2 * D_FF).astype(F32), w_down, l, S)
    return xf.reshape(B, S, D)
```
